```python
import math
import jax, jax.numpy as jnp
from jax import lax
import numpy as np

D_MODEL = 1024
BATCH = 8
SEQ = 8192
DEPTH = 1

ROPE_THETA = 10000.0
EPS = 1e-6
Q_BLOCK = 128

MLA_HEADS = 4
MLA_Q_RANK = 256
MLA_KV_RANK = 256
MLA_NOPE_DIM = 128
MLA_ROPE_DIM = 64
MLA_V_DIM = 128

DIFF_HEADS = 4
DIFF_QK_DIM = 64
DIFF_V_DIM = 2 * DIFF_QK_DIM

MIX_WIDTH = MLA_HEADS * MLA_V_DIM + DIFF_HEADS * DIFF_V_DIM
IN_SPLITS = (MLA_Q_RANK, MLA_KV_RANK, MLA_ROPE_DIM,
             DIFF_HEADS * 2 * DIFF_QK_DIM, DIFF_HEADS * 2 * DIFF_QK_DIM, DIFF_HEADS * DIFF_V_DIM)
IN_COLS = sum(IN_SPLITS)

N_GROUPS = 4
EXPERTS_PER_GROUP = 8
N_EXPERTS = N_GROUPS * EXPERTS_PER_GROUP
TOP_K = 2
D_EXPERT = 512
MOE_BLOCK = 128

kernel_name = 'hymba_mla_diffattn_hiermoe_encoder'


def rms_norm(x, g):
    xf = x.astype(jnp.float32)
    y = xf * lax.rsqrt(jnp.mean(xf * xf, axis=-1, keepdims=True) + EPS)
    return (y * g.astype(jnp.float32)).astype(x.dtype)


def rope(x, pos):
    d = x.shape[-1]
    inv = ROPE_THETA ** (-jnp.arange(0, d, 2, dtype=jnp.float32) / d)
    ang = pos.astype(jnp.float32)[:, None] * inv[None, :]
    shape = (1, x.shape[1]) + (1,) * (x.ndim - 3) + (d // 2,)
    cos = jnp.cos(ang).reshape(shape)
    sin = jnp.sin(ang).reshape(shape)
    xf = x.astype(jnp.float32)
    x1, x2 = xf[..., : d // 2], xf[..., d // 2:]
    return jnp.concatenate([x1 * cos - x2 * sin, x2 * cos + x1 * sin], axis=-1).astype(x.dtype)


def to_blocks(t):
    b, s = t.shape[0], t.shape[1]
    return t.reshape((b, s // Q_BLOCK, Q_BLOCK) + t.shape[2:]).swapaxes(0, 1)


def from_blocks(t):
    nb, b, qb = t.shape[0], t.shape[1], t.shape[2]
    return t.swapaxes(0, 1).reshape((b, nb * qb) + t.shape[3:])


def mla_attention(q_nope, q_pe, k_nope, k_pe, v):
    scale = (q_nope.shape[-1] + q_pe.shape[-1]) ** -0.5

    def block(args):
        qn, qr = args
        s = (jnp.einsum('bqhd,bkhd->bhqk', qn, k_nope)
             + jnp.einsum('bqhr,bkr->bhqk', qr, k_pe)).astype(jnp.float32) * scale
        p = jax.nn.softmax(s, axis=-1).astype(v.dtype)
        return jnp.einsum('bhqk,bkhd->bqhd', p, v)

    return from_blocks(lax.map(block, (to_blocks(q_nope), to_blocks(q_pe))))


def diff_attention(q1, q2, k1, k2, v, lam):
    scale = q1.shape[-1] ** -0.5

    def block(args):
        a1, a2 = args
        s1 = jnp.einsum('bqhd,bkhd->bhqk', a1, k1).astype(jnp.float32) * scale
        s2 = jnp.einsum('bqhd,bkhd->bhqk', a2, k2).astype(jnp.float32) * scale
        p = (jax.nn.softmax(s1, axis=-1) - lam * jax.nn.softmax(s2, axis=-1)).astype(v.dtype)
        return jnp.einsum('bhqk,bkhd->bqhd', p, v)

    return from_blocks(lax.map(block, (to_blocks(q1), to_blocks(q2))))


def hier_moe(h, w_rg, b_rg, w_re, b_re, w_gate, w_up, w_down):
    b, s, d = h.shape
    t = b * s
    a = t * TOP_K
    hf = h.reshape(t, d)
    hf32 = hf.astype(jnp.float32)
    g_logits = hf32 @ w_rg.astype(jnp.float32) + b_rg.astype(jnp.float32)
    g_prob, g_idx = lax.top_k(jax.nn.softmax(g_logits, axis=-1), 1)
    e_logits = (hf32 @ w_re.astype(jnp.float32) + b_re.astype(jnp.float32)).reshape(t, N_GROUPS, EXPERTS_PER_GROUP)
    e_sel = jnp.take_along_axis(e_logits, g_idx[:, :, None], axis=1)[:, 0]
    top_l, top_i = lax.top_k(e_sel, TOP_K)
    weights = (g_prob * jax.nn.softmax(top_l, axis=-1)).astype(h.dtype)
    eid = g_idx * EXPERTS_PER_GROUP + top_i

    e_flat = eid.reshape(a).astype(jnp.int32)
    t_flat = jnp.repeat(jnp.arange(t, dtype=jnp.int32), TOP_K)
    w_flat = weights.reshape(a)
    order = jnp.argsort(e_flat)
    e_s, t_s, w_s = e_flat[order], t_flat[order], w_flat[order]
    counts = jnp.zeros((N_EXPERTS,), jnp.int32).at[e_flat].add(1)
    starts = jnp.cumsum(counts) - counts
    padded = ((counts + MOE_BLOCK - 1) // MOE_BLOCK) * MOE_BLOCK
    pends = jnp.cumsum(padded)
    pstarts = pends - padded
    dest = pstarts[e_s] + (jnp.arange(a, dtype=jnp.int32) - starts[e_s])
    rows = a + N_EXPERTS * MOE_BLOCK
    nb = rows // MOE_BLOCK
    row_token = jnp.zeros((rows,), jnp.int32).at[dest].set(t_s)
    row_weight = jnp.zeros((rows,), h.dtype).at[dest].set(w_s)
    block_expert = jnp.minimum(
        jnp.searchsorted(pends, jnp.arange(nb, dtype=jnp.int32) * MOE_BLOCK, side='right'),
        N_EXPERTS - 1).astype(jnp.int32)
    xs = hf[row_token].reshape(nb, MOE_BLOCK, d)

    def expert_block(args):
        xb, e = args
        g = xb @ w_gate[e]
        u = xb @ w_up[e]
        return (jax.nn.silu(g) * u) @ w_down[e]

    ys = lax.map(expert_block, (xs, block_expert)).reshape(rows, d)
    out = jnp.zeros((t, d), h.dtype).at[row_token].add(ys * row_weight[:, None])
    return out.reshape(b, s, d)


def setup_inputs(seed: int = 0) -> dict:
    key = jax.random.key(seed)
    ks = jax.random.split(key, 32)
    f32 = jnp.float32
    L, D = DEPTH, D_MODEL

    def nrm(k, shape, std):
        return jax.random.normal(k, shape, f32) * std

    def gain(k, shape):
        return 1.0 + 0.02 * jax.random.normal(k, shape, f32)

    return {
        'x': nrm(ks[0], (BATCH, SEQ, D), 1.0),
        'c': nrm(ks[1], (BATCH, D), 1.0),
        'w_ada': nrm(ks[2], (L, D, 6 * D), 0.5 * D ** -0.5),
        'b_ada': nrm(ks[3], (L, 6 * D), 0.02),
        'norm1_g': gain(ks[4], (L, D)),
        'w_in': nrm(ks[5], (L, D, IN_COLS), D ** -0.5),
        'q_a_norm_g': gain(ks[6], (L, MLA_Q_RANK)),
        'w_q_up': nrm(ks[7], (L, MLA_Q_RANK, MLA_HEADS * (MLA_NOPE_DIM + MLA_ROPE_DIM)), MLA_Q_RANK ** -0.5),
        'kv_a_norm_g': gain(ks[8], (L, MLA_KV_RANK)),
        'w_kv_up': nrm(ks[9], (L, MLA_KV_RANK, MLA_HEADS * (MLA_NOPE_DIM + MLA_V_DIM)), MLA_KV_RANK ** -0.5),
        'lambda_q1': nrm(ks[10], (L, DIFF_QK_DIM), 0.1),
        'lambda_k1': nrm(ks[11], (L, DIFF_QK_DIM), 0.1),
        'lambda_q2': nrm(ks[12], (L, DIFF_QK_DIM), 0.1),
        'lambda_k2': nrm(ks[13], (L, DIFF_QK_DIM), 0.1),
        'subln_g': gain(ks[14], (L, DIFF_V_DIM)),
        'w_out': nrm(ks[15], (L, MIX_WIDTH, D), MIX_WIDTH ** -0.5),
        'norm2_g': gain(ks[16], (L, D)),
        'w_router_group': nrm(ks[17], (L, D, N_GROUPS), D ** -0.5),
        'b_router_group': nrm(ks[18], (L, N_GROUPS), 0.01),
        'w_router_expert': nrm(ks[19], (L, D, N_EXPERTS), D ** -0.5),
        'b_router_expert': nrm(ks[20], (L, N_EXPERTS), 0.01),
        'w_expert_gate': nrm(ks[21], (L, N_EXPERTS, D, D_EXPERT), D ** -0.5),
        'w_expert_up': nrm(ks[22], (L, N_EXPERTS, D, D_EXPERT), D ** -0.5),
        'w_expert_down': nrm(ks[23], (L, N_EXPERTS, D_EXPERT, D), D_EXPERT ** -0.5),
        'final_norm_g': gain(ks[24], (D,)),
    }


def reference(x, c, w_ada, b_ada, norm1_g, w_in, q_a_norm_g, w_q_up, kv_a_norm_g, w_kv_up,
              lambda_q1, lambda_k1, lambda_q2, lambda_k2, subln_g, w_out, norm2_g,
              w_router_group, b_router_group, w_router_expert, b_router_expert,
              w_expert_gate, w_expert_up, w_expert_down, final_norm_g):
    b, s, _ = x.shape
    pos = jnp.arange(s, dtype=jnp.int32)
    offsets = [int(v) for v in np.cumsum(IN_SPLITS)[:-1]]
    for l in range(DEPTH):
        mod = jax.nn.silu(c) @ w_ada[l] + b_ada[l]
        shift1, scale1, gate1, shift2, scale2, gate2 = jnp.split(mod, 6, axis=-1)

        h = rms_norm(x, norm1_g[l]) * (1.0 + scale1[:, None, :]) + shift1[:, None, :]
        z = h @ w_in[l]
        c_q, c_kv, k_pe, dq, dk, dv = jnp.split(z, offsets, axis=-1)

        q = (rms_norm(c_q, q_a_norm_g[l]) @ w_q_up[l]).reshape(b, s, MLA_HEADS, MLA_NOPE_DIM + MLA_ROPE_DIM)
        q_nope, q_pe = q[..., :MLA_NOPE_DIM], rope(q[..., MLA_NOPE_DIM:], pos)
        kv = (rms_norm(c_kv, kv_a_norm_g[l]) @ w_kv_up[l]).reshape(b, s, MLA_HEADS, MLA_NOPE_DIM + MLA_V_DIM)
        k_nope, v_mla = kv[..., :MLA_NOPE_DIM], kv[..., MLA_NOPE_DIM:]
        k_pe = rope(k_pe[:, :, None, :], pos)[:, :, 0, :]
        mla_out = mla_attention(q_nope, q_pe, k_nope, k_pe, v_mla).reshape(b, s, MLA_HEADS * MLA_V_DIM)

        dq = rope(dq.reshape(b, s, DIFF_HEADS, 2, DIFF_QK_DIM), pos)
        dk = rope(dk.reshape(b, s, DIFF_HEADS, 2, DIFF_QK_DIM), pos)
        dv = dv.reshape(b, s, DIFF_HEADS, DIFF_V_DIM)
        lambda_init = 0.8 - 0.6 * math.exp(-0.3 * l)
        lam = (jnp.exp(jnp.sum(lambda_q1[l].astype(jnp.float32) * lambda_k1[l].astype(jnp.float32)))
               - jnp.exp(jnp.sum(lambda_q2[l].astype(jnp.float32) * lambda_k2[l].astype(jnp.float32)))
               + lambda_init)
        d_out = diff_attention(dq[..., 0, :], dq[..., 1, :], dk[..., 0, :], dk[..., 1, :], dv, lam)
        d_out = (rms_norm(d_out, subln_g[l]) * (1.0 - lambda_init)).reshape(b, s, DIFF_HEADS * DIFF_V_DIM)

        mix = jnp.concatenate([mla_out, d_out], axis=-1) @ w_out[l]
        x = x + gate1[:, None, :] * mix

        h2 = rms_norm(x, norm2_g[l]) * (1.0 + scale2[:, None, :]) + shift2[:, None, :]
        moe = hier_moe(h2, w_router_group[l], b_router_group[l], w_router_expert[l], b_router_expert[l],
                       w_expert_gate[l], w_expert_up[l], w_expert_down[l])
        x = x + gate2[:, None, :] * moe
    return rms_norm(x, final_norm_g)
```

```python
import functools
import math

import jax
import jax.numpy as jnp
from jax import lax
from jax.experimental import pallas as pl
from jax.experimental.pallas import tpu as pltpu

D_MODEL = 1024
ROPE_THETA = 10000.0
EPS = 1e-6

MLA_HEADS = 4
MLA_Q_RANK = 256
MLA_KV_RANK = 256
MLA_NOPE_DIM = 128
MLA_ROPE_DIM = 64
MLA_V_DIM = 128
MLA_QK_PAD = 256

DIFF_HEADS = 4
DIFF_QK_DIM = 64
DIFF_V_DIM = 128

N_GROUPS = 4
EXPERTS_PER_GROUP = 8
N_EXPERTS = 32
TOP_K = 2
D_EXPERT = 512

LANES = 128
ROW_CHUNKS = D_MODEL // LANES
ROUTER_ROWS = 40

VMEM_LIMIT = 56 * 1024 * 1024

PRE_TM = 512
ATT_TQ = 512
ATT_TK = 512
POST_TM = 512
DISPATCH_TM = 512
EXPERT_TM = 256
COMBINE_TM = 256

F32 = jnp.float32
BF16 = jnp.bfloat16
NEG_INF = float("-inf")


def _cparams(sem):
    return pltpu.CompilerParams(dimension_semantics=sem, vmem_limit_bytes=VMEM_LIMIT)


def _ada_kernel(c_ref, w_ref, b_ref, o_ref):
    c = c_ref[...]
    sc = c / (1.0 + jnp.exp(-c))
    o_ref[...] = jnp.dot(sc, w_ref[...], preferred_element_type=F32,
                         precision=lax.Precision.HIGHEST) + b_ref[...]


def _ada(c, w_ada, b_ada):
    b, d = c.shape
    n = w_ada.shape[1]
    tn = 1536
    return pl.pallas_call(
        _ada_kernel,
        out_shape=jax.ShapeDtypeStruct((b, n), F32),
        grid=(n // tn,),
        in_specs=[pl.BlockSpec((b, d), lambda j: (0, 0)),
                  pl.BlockSpec((d, tn), lambda j: (0, j)),
                  pl.BlockSpec((1, tn), lambda j: (0, j))],
        out_specs=pl.BlockSpec((b, tn), lambda j: (0, j)),
        compiler_params=_cparams(("parallel",)),
        name="ada",
    )(c, w_ada, b_ada.reshape(1, n))


def _rms(x, g):
    return x * lax.rsqrt(jnp.mean(x * x, axis=-1, keepdims=True) + EPS) * g


def _rope_tile(x, cos, sin_signed, first_half):
    fwd = pltpu.roll(x, 96, axis=1)
    bwd = pltpu.roll(x, 32, axis=1)
    return x * cos + jnp.where(first_half, fwd, bwd) * sin_signed


def _pre_kernel(x_ref, mod_ref, g1_ref, win_ref, gq_ref, wq_ref, gkv_ref, wkv_ref, cos_ref, sin_ref,
                qm_ref, km_ref, vm_ref, dq_ref, dk_ref, dv_ref):
    x = x_ref[0]
    shift1 = mod_ref[0, 0:1, :]
    scale1 = mod_ref[0, 1:2, :]
    h = _rms(x, g1_ref[...]) * (1.0 + scale1) + shift1
    z = jnp.dot(h.astype(BF16), win_ref[...], preferred_element_type=F32)

    cos = cos_ref[...]
    sin = sin_ref[...]
    lane = lax.broadcasted_iota(jnp.int32, cos.shape, 1)
    first_half = (lane % 64) < 32

    cq = _rms(z[:, 0:256], gq_ref[...])
    q = jnp.dot(cq.astype(BF16), wq_ref[...], preferred_element_type=F32)
    ckv = _rms(z[:, 256:512], gkv_ref[...])
    kv = jnp.dot(ckv.astype(BF16), wkv_ref[...], preferred_element_type=F32)
    kpe = _rope_tile(z[:, 2048:2176], cos, sin, first_half).astype(BF16)

    q_scale = (MLA_NOPE_DIM + MLA_ROPE_DIM) ** -0.5
    for hd in range(MLA_HEADS):
        qn = q[:, hd * 128:(hd + 1) * 128] * q_scale
        qp = _rope_tile(q[:, 512 + hd * 128:512 + (hd + 1) * 128], cos, sin, first_half) * q_scale
        qm_ref[0, hd, :, 0:128] = qn.astype(BF16)
        qm_ref[0, hd, :, 128:256] = qp.astype(BF16)
        km_ref[0, hd, :, 0:128] = kv[:, hd * 128:(hd + 1) * 128].astype(BF16)
        km_ref[0, hd, :, 128:256] = kpe
        vm_ref[0, hd] = kv[:, 512 + hd * 128:512 + (hd + 1) * 128].astype(BF16)

    d_scale = DIFF_QK_DIM ** -0.5
    for hd in range(DIFF_HEADS):
        dq = _rope_tile(z[:, 512 + hd * 128:512 + (hd + 1) * 128], cos, sin, first_half) * d_scale
        dk = _rope_tile(z[:, 1024 + hd * 128:1024 + (hd + 1) * 128], cos, sin, first_half)
        dq_ref[0, hd] = dq.astype(BF16)
        dk_ref[0, hd] = dk.astype(BF16)
        dv_ref[0, hd] = z[:, 1536 + hd * 128:1536 + (hd + 1) * 128].astype(BF16)


def _pre(x, mod, g1, win, gq, wq, gkv, wkv, cos_t, sin_t):
    b, s, d = x.shape
    tm = PRE_TM
    full = lambda a: pl.BlockSpec(a.shape, lambda bi, i: (0,) * a.ndim)
    head_out = lambda w: pl.BlockSpec((1, 4, tm, w), lambda bi, i: (bi, 0, i, 0))
    head_shape = lambda w: jax.ShapeDtypeStruct((b, 4, s, w), BF16)
    return pl.pallas_call(
        _pre_kernel,
        out_shape=(head_shape(256), head_shape(256), head_shape(128),
                   head_shape(128), head_shape(128), head_shape(128)),
        grid=(b, s // tm),
        in_specs=[pl.BlockSpec((1, tm, d), lambda bi, i: (bi, i, 0)),
                  pl.BlockSpec((1, 6, d), lambda bi, i: (bi, 0, 0)),
                  full(g1), full(win), full(gq), full(wq), full(gkv), full(wkv),
                  pl.BlockSpec((tm, LANES), lambda bi, i: (i, 0)),
                  pl.BlockSpec((tm, LANES), lambda bi, i: (i, 0))],
        out_specs=(head_out(256), head_out(256), head_out(128),
                   head_out(128), head_out(128), head_out(128)),
        compiler_params=_cparams(("parallel", "parallel")),
        name="pre",
    )(x, mod, g1, win, gq, wq, gkv, wkv, cos_t, sin_t)


def _online_step(q, k, v, m, l, acc):
    s = lax.dot_general(q, k, (((1,), (1,)), ((), ())), preferred_element_type=F32)
    m_new = jnp.maximum(m, jnp.max(s, axis=-1, keepdims=True))
    alpha = jnp.exp(m - m_new)
    p = jnp.exp(s - m_new)
    l_new = alpha * l + jnp.sum(p, axis=-1, keepdims=True)
    acc_new = alpha * acc + jnp.dot(p.astype(BF16), v, preferred_element_type=F32)
    return m_new, l_new, acc_new


def _mla_kernel(q_ref, k_ref, v_ref, o_ref):
    tq = q_ref.shape[2]
    s_len = k_ref.shape[2]
    q = q_ref[0, 0]

    def body(j, carry):
        off = pl.multiple_of(j * ATT_TK, ATT_TK)
        k = k_ref[0, 0, pl.ds(off, ATT_TK), :]
        v = v_ref[0, 0, pl.ds(off, ATT_TK), :]
        return _online_step(q, k, v, *carry)

    init = (jnp.full((tq, 1), NEG_INF, F32), jnp.zeros((tq, 1), F32), jnp.zeros((tq, MLA_V_DIM), F32))
    _, l, acc = lax.fori_loop(0, s_len // ATT_TK, body, init)
    o_ref[0] = (acc / l).astype(o_ref.dtype)


def _mla_attn(q, k, v):
    b, h, s, _ = q.shape
    tq = ATT_TQ
    return pl.pallas_call(
        _mla_kernel,
        out_shape=jax.ShapeDtypeStruct((b, s, h * MLA_V_DIM), BF16),
        grid=(b, h, s // tq),
        in_specs=[pl.BlockSpec((1, 1, tq, MLA_QK_PAD), lambda bi, hi, i: (bi, hi, i, 0)),
                  pl.BlockSpec((1, 1, s, MLA_QK_PAD), lambda bi, hi, i: (bi, hi, 0, 0)),
                  pl.BlockSpec((1, 1, s, MLA_V_DIM), lambda bi, hi, i: (bi, hi, 0, 0))],
        out_specs=pl.BlockSpec((1, tq, MLA_V_DIM), lambda bi, hi, i: (bi, i, hi)),
        compiler_params=_cparams(("parallel", "parallel", "parallel")),
        name="mla_attn",
    )(q, k, v)


def _diff_kernel(lam_ref, g_ref, q_ref, k_ref, v_ref, o_ref, *, out_scale, lambda_init):
    tq = q_ref.shape[2]
    s_len = k_ref.shape[2]
    q = q_ref[0, 0]
    lane = lax.broadcasted_iota(jnp.int32, q.shape, 1)
    zero = jnp.zeros_like(q)
    q1 = jnp.where(lane < DIFF_QK_DIM, q, zero)
    q2 = jnp.where(lane < DIFF_QK_DIM, zero, q)

    def body(j, carry):
        off = pl.multiple_of(j * ATT_TK, ATT_TK)
        k = k_ref[0, 0, pl.ds(off, ATT_TK), :]
        v = v_ref[0, 0, pl.ds(off, ATT_TK), :]
        c1 = _online_step(q1, k, v, *carry[:3])
        c2 = _online_step(q2, k, v, *carry[3:])
        return c1 + c2

    one = (jnp.full((tq, 1), NEG_INF, F32), jnp.zeros((tq, 1), F32), jnp.zeros((tq, DIFF_V_DIM), F32))
    _, l1, a1, _, l2, a2 = lax.fori_loop(0, s_len // ATT_TK, body, one + one)

    lv = lam_ref[...]
    lam = (jnp.exp(jnp.sum(lv[0:1] * lv[1:2], axis=-1, keepdims=True))
           - jnp.exp(jnp.sum(lv[2:3] * lv[3:4], axis=-1, keepdims=True)) + lambda_init)
    o = a1 / l1 - lam * (a2 / l2)
    o_ref[0] = (_rms(o, g_ref[...]) * out_scale).astype(o_ref.dtype)


def _diff_attn(lam_vecs, subln_g, q, k, v, lambda_init):
    b, h, s, _ = q.shape
    tq = ATT_TQ
    kern = functools.partial(_diff_kernel, out_scale=1.0 - lambda_init, lambda_init=lambda_init)
    return pl.pallas_call(
        kern,
        out_shape=jax.ShapeDtypeStruct((b, s, h * DIFF_V_DIM), BF16),
        grid=(b, h, s // tq),
        in_specs=[pl.BlockSpec((4, DIFF_QK_DIM), lambda bi, hi, i: (0, 0)),
                  pl.BlockSpec((1, DIFF_V_DIM), lambda bi, hi, i: (0, 0)),
                  pl.BlockSpec((1, 1, tq, 128), lambda bi, hi, i: (bi, hi, i, 0)),
                  pl.BlockSpec((1, 1, s, 128), lambda bi, hi, i: (bi, hi, 0, 0)),
                  pl.BlockSpec((1, 1, s, DIFF_V_DIM), lambda bi, hi, i: (bi, hi, 0, 0))],
        out_specs=pl.BlockSpec((1, tq, DIFF_V_DIM), lambda bi, hi, i: (bi, i, hi)),
        compiler_params=_cparams(("parallel", "parallel", "parallel")),
        name="diff_attn",
    )(lam_vecs, subln_g, q, k, v)


def _post_kernel(x_ref, mla_ref, dif_ref, mod_ref, wo_ref, g2_ref, wr_ref, br_ref,
                 x1_ref, h2_ref, ids_ref, ws_ref, cnt_ref, base_ref):
    first = jnp.logical_and(pl.program_id(0) == 0, pl.program_id(1) == 0)

    @pl.when(first)
    def _():
        base_ref[...] = jnp.zeros_like(base_ref)

    tm = x_ref.shape[1]
    gate1 = mod_ref[0, 2:3, :]
    shift2 = mod_ref[0, 3:4, :]
    scale2 = mod_ref[0, 4:5, :]
    mixin = jnp.concatenate([mla_ref[0], dif_ref[0]], axis=-1)
    mix = jnp.dot(mixin, wo_ref[...], preferred_element_type=F32)
    x1 = x_ref[0] + gate1 * mix
    x1_ref[0] = x1
    h2 = _rms(x1, g2_ref[...]) * (1.0 + scale2) + shift2
    for c in range(ROW_CHUNKS):
        h2_ref[:, c, :] = h2[:, c * LANES:(c + 1) * LANES]

    logits = lax.dot_general(wr_ref[...], h2, (((1,), (1,)), ((), ())), preferred_element_type=F32,
                             precision=lax.Precision.HIGHEST) + br_ref[...]
    el = logits[0:N_EXPERTS]
    gl = logits[N_EXPERTS:N_EXPERTS + N_GROUPS]

    grow = lax.broadcasted_iota(jnp.int32, gl.shape, 0).astype(F32)
    gmax = jnp.max(gl, axis=0, keepdims=True)
    gidx = jnp.min(jnp.where(gl == gmax, grow, float(N_GROUPS)), axis=0, keepdims=True)
    gprob = 1.0 / jnp.sum(jnp.exp(gl - gmax), axis=0, keepdims=True)

    erow_i = lax.broadcasted_iota(jnp.int32, el.shape, 0)
    erow = erow_i.astype(F32)
    egrp = (erow_i // EXPERTS_PER_GROUP).astype(F32)
    masked = jnp.where(egrp == gidx, el, NEG_INF)
    l1 = jnp.max(masked, axis=0, keepdims=True)
    i1 = jnp.min(jnp.where(masked == l1, erow, float(N_EXPERTS)), axis=0, keepdims=True)
    oh1 = erow == i1
    masked2 = jnp.where(oh1, NEG_INF, masked)
    l2 = jnp.max(masked2, axis=0, keepdims=True)
    i2 = jnp.min(jnp.where(masked2 == l2, erow, float(N_EXPERTS)), axis=0, keepdims=True)
    oh2 = erow == i2
    e21 = jnp.exp(l2 - l1)
    w1 = gprob * (1.0 / (1.0 + e21))
    w2 = gprob * (e21 / (1.0 + e21))

    oh = jnp.where(jnp.logical_or(oh1, oh2), 1.0, 0.0)
    tr = lax.broadcasted_iota(jnp.int32, (tm, tm), 0)
    tc = lax.broadcasted_iota(jnp.int32, (tm, tm), 1)
    upper = jnp.where(tr < tc, 1.0, 0.0).astype(BF16)
    pos = base_ref[...][:, 0:1] + jnp.dot(oh.astype(BF16), upper, preferred_element_type=F32)
    r1 = jnp.sum(jnp.where(oh1, pos, 0.0), axis=0, keepdims=True)
    r2 = jnp.sum(jnp.where(oh2, pos, 0.0), axis=0, keepdims=True)
    base_new = base_ref[...] + jnp.sum(oh, axis=1, keepdims=True)
    base_ref[...] = base_new
    cnt_ref[...] = base_new

    ids_ref[0] = jnp.concatenate([i1, i2, r1, r2], axis=0).astype(jnp.int32)
    ws_ref[0] = jnp.concatenate([w1, w2], axis=0)


def _post(x, mla_out, d_out, mod, wo, g2, wr, br):
    b, s, d = x.shape
    tm = POST_TM
    nt = s // tm
    full = lambda a: pl.BlockSpec(a.shape, lambda bi, i: (0,) * a.ndim)
    tok = lambda w: pl.BlockSpec((1, tm, w), lambda bi, i: (bi, i, 0))
    return pl.pallas_call(
        _post_kernel,
        out_shape=(jax.ShapeDtypeStruct((b, s, d), F32),
                   jax.ShapeDtypeStruct((b * s, ROW_CHUNKS, LANES), F32),
                   jax.ShapeDtypeStruct((b * nt, 4, tm), jnp.int32),
                   jax.ShapeDtypeStruct((b * nt, 2, tm), F32),
                   jax.ShapeDtypeStruct((N_EXPERTS, LANES), F32)),
        grid=(b, nt),
        in_specs=[tok(d), tok(mla_out.shape[2]), tok(d_out.shape[2]),
                  pl.BlockSpec((1, 6, d), lambda bi, i: (bi, 0, 0)),
                  full(wo), full(g2), full(wr), full(br)],
        out_specs=(tok(d),
                   pl.BlockSpec((tm, ROW_CHUNKS, LANES), lambda bi, i: (bi * nt + i, 0, 0)),
                   pl.BlockSpec((1, 4, tm), lambda bi, i: (bi * nt + i, 0, 0)),
                   pl.BlockSpec((1, 2, tm), lambda bi, i: (bi * nt + i, 0, 0)),
                   pl.BlockSpec((N_EXPERTS, LANES), lambda bi, i: (0, 0))),
        scratch_shapes=[pltpu.VMEM((N_EXPERTS, LANES), F32)],
        compiler_params=_cparams(("arbitrary", "arbitrary")),
        name="post",
    )(x, mla_out, d_out, mod, wo, g2, wr, br)


def _dispatch_kernel(d0_ref, d1_ref, pad_ref, h_ref, xs_ref, zero_ref, sem):
    tm = h_ref.shape[0]
    i = pl.program_id(0)

    @pl.when(i == 0)
    def _():
        zero_ref[...] = jnp.zeros_like(zero_ref)
        for e in range(N_EXPERTS):
            pltpu.make_async_copy(zero_ref, xs_ref.at[pl.ds(pad_ref[e], EXPERT_TM)], sem).start()
        for e in range(N_EXPERTS):
            pltpu.make_async_copy(zero_ref, xs_ref.at[pl.ds(pad_ref[e], EXPERT_TM)], sem).wait()

    base = i * tm

    def issue(r, carry):
        pltpu.make_async_copy(h_ref.at[r], xs_ref.at[d0_ref[base + r]], sem).start()
        pltpu.make_async_copy(h_ref.at[r], xs_ref.at[d1_ref[base + r]], sem).start()
        return carry

    lax.fori_loop(0, tm, issue, 0)

    def drain(r, carry):
        pltpu.make_async_copy(h_ref.at[r], xs_ref.at[0], sem).wait()
        pltpu.make_async_copy(h_ref.at[r], xs_ref.at[0], sem).wait()
        return carry

    lax.fori_loop(0, tm, drain, 0)


def _dispatch(dest0, dest1, pad_start, h2, rows):
    t = h2.shape[0]
    tm = DISPATCH_TM
    return pl.pallas_call(
        _dispatch_kernel,
        out_shape=jax.ShapeDtypeStruct((rows, ROW_CHUNKS, LANES), F32),
        grid_spec=pltpu.PrefetchScalarGridSpec(
            num_scalar_prefetch=3,
            grid=(t // tm,),
            in_specs=[pl.BlockSpec((tm, ROW_CHUNKS, LANES), lambda i, *_: (i, 0, 0))],
            out_specs=pl.BlockSpec(memory_space=pl.ANY),
            scratch_shapes=[pltpu.VMEM((EXPERT_TM, ROW_CHUNKS, LANES), F32), pltpu.SemaphoreType.DMA]),
        compiler_params=_cparams(("arbitrary",)),
        name="dispatch",
    )(dest0, dest1, pad_start, h2)


def _expert_kernel(be_ref, nbu_ref, x_ref, wg_ref, wu_ref, wd_ref, y_ref):
    @pl.when(pl.program_id(0) < nbu_ref[0])
    def _():
        x = jnp.concatenate([x_ref[:, c, :] for c in range(ROW_CHUNKS)], axis=-1).astype(BF16)
        g = jnp.dot(x, wg_ref[0], preferred_element_type=F32)
        u = jnp.dot(x, wu_ref[0], preferred_element_type=F32)
        a = (g / (1.0 + jnp.exp(-g))) * u
        y = jnp.dot(a.astype(BF16), wd_ref[0], preferred_element_type=F32)
        for c in range(ROW_CHUNKS):
            y_ref[:, c, :] = y[:, c * LANES:(c + 1) * LANES]


def _experts(block_expert, nb_used, xs, wg, wu, wd):
    rows = xs.shape[0]
    d = D_MODEL
    tm = EXPERT_TM
    nb = rows // tm
    row_map = lambda j, be, nbu: (jnp.minimum(j, nbu[0] - 1), 0, 0)
    w_map = lambda j, be, nbu: (be[j], 0, 0)
    return pl.pallas_call(
        _expert_kernel,
        out_shape=jax.ShapeDtypeStruct((rows, ROW_CHUNKS, LANES), F32),
        grid_spec=pltpu.PrefetchScalarGridSpec(
            num_scalar_prefetch=2,
            grid=(nb,),
            in_specs=[pl.BlockSpec((tm, ROW_CHUNKS, LANES), row_map),
                      pl.BlockSpec((1, d, D_EXPERT), w_map),
                      pl.BlockSpec((1, d, D_EXPERT), w_map),
                      pl.BlockSpec((1, D_EXPERT, d), w_map)],
            out_specs=pl.BlockSpec((tm, ROW_CHUNKS, LANES), row_map)),
        compiler_params=_cparams(("arbitrary",)),
        name="experts",
    )(block_expert, nb_used, xs, wg, wu, wd)


def _combine_kernel(d0_ref, d1_ref, x1_ref, w_ref, mod_ref, gf_ref, ys_ref, o_ref, y0_ref, y1_ref, sem):
    tm = x1_ref.shape[0]
    base = pl.program_id(0) * tm

    def issue(r, carry):
        pltpu.make_async_copy(ys_ref.at[d0_ref[base + r]], y0_ref.at[r], sem).start()
        pltpu.make_async_copy(ys_ref.at[d1_ref[base + r]], y1_ref.at[r], sem).start()
        return carry

    lax.fori_loop(0, tm, issue, 0)

    def drain(r, carry):
        pltpu.make_async_copy(ys_ref.at[0], y0_ref.at[r], sem).wait()
        pltpu.make_async_copy(ys_ref.at[0], y1_ref.at[r], sem).wait()
        return carry

    lax.fori_loop(0, tm, drain, 0)

    gate2 = mod_ref[0, 5:6, :]
    w = w_ref[...]
    w0 = w[:, 0:1]
    w1 = w[:, 1:2]
    moe = jnp.concatenate([w0 * y0_ref[:, c, :] + w1 * y1_ref[:, c, :] for c in range(ROW_CHUNKS)], axis=-1)
    x2 = x1_ref[...] + gate2 * moe
    o_ref[...] = _rms(x2, gf_ref[...])


def _combine(dest0, dest1, x1, w_tok, mod, gf, ys, s_len):
    t, d = x1.shape
    tm = COMBINE_TM
    per_batch = s_len // tm
    return pl.pallas_call(
        _combine_kernel,
        out_shape=jax.ShapeDtypeStruct((t, d), F32),
        grid_spec=pltpu.PrefetchScalarGridSpec(
            num_scalar_prefetch=2,
            grid=(t // tm,),
            in_specs=[pl.BlockSpec((tm, d), lambda i, *_: (i, 0)),
                      pl.BlockSpec((tm, TOP_K), lambda i, *_: (i, 0)),
                      pl.BlockSpec((1, 6, d), lambda i, *_: (i // per_batch, 0, 0)),
                      pl.BlockSpec((1, d), lambda i, *_: (0, 0)),
                      pl.BlockSpec(memory_space=pl.ANY)],
            out_specs=pl.BlockSpec((tm, d), lambda i, *_: (i, 0)),
            scratch_shapes=[pltpu.VMEM((tm, ROW_CHUNKS, LANES), F32), pltpu.VMEM((tm, ROW_CHUNKS, LANES), F32),
                            pltpu.SemaphoreType.DMA]),
        compiler_params=_cparams(("arbitrary",)),
        name="combine",
    )(dest0, dest1, x1, w_tok, mod, gf, ys)


def _rope_tables(s_len):
    half = MLA_ROPE_DIM // 2
    inv = ROPE_THETA ** (-jnp.arange(0, MLA_ROPE_DIM, 2, dtype=F32) / MLA_ROPE_DIM)
    ang = jnp.arange(s_len, dtype=jnp.int32).astype(F32)[:, None] * inv[None, :]
    cos, sin = jnp.cos(ang), jnp.sin(ang)
    reps = LANES // MLA_ROPE_DIM
    cos_t = jnp.tile(jnp.concatenate([cos, cos], axis=1), (1, reps))
    sin_t = jnp.tile(jnp.concatenate([-sin, sin], axis=1), (1, reps))
    del half
    return cos_t, sin_t


def _layer(l, x, mod, norm1_g, w_in, q_a_norm_g, w_q_up, kv_a_norm_g, w_kv_up, lambda_q1, lambda_k1,
           lambda_q2, lambda_k2, subln_g, w_out, norm2_g, w_router_group, b_router_group, w_router_expert,
           b_router_expert, w_expert_gate, w_expert_up, w_expert_down, final_norm_g, cos_t, sin_t):
    b, s, d = x.shape
    t = b * s

    wi = w_in[l]
    win = jnp.concatenate([wi[:, 0:512], wi[:, 576:2112], wi[:, 512:576], jnp.zeros((d, 64), F32)],
                          axis=1).astype(BF16)
    wq4 = w_q_up[l].reshape(MLA_Q_RANK, MLA_HEADS, MLA_NOPE_DIM + MLA_ROPE_DIM)
    wq_pe = jnp.pad(wq4[:, :, MLA_NOPE_DIM:], ((0, 0), (0, 0), (0, 64)))
    wq = jnp.concatenate([wq4[:, :, :MLA_NOPE_DIM].reshape(MLA_Q_RANK, 512),
                          wq_pe.reshape(MLA_Q_RANK, 512)], axis=1).astype(BF16)
    wkv4 = w_kv_up[l].reshape(MLA_KV_RANK, MLA_HEADS, MLA_NOPE_DIM + MLA_V_DIM)
    wkv = jnp.concatenate([wkv4[:, :, :MLA_NOPE_DIM].reshape(MLA_KV_RANK, 512),
                           wkv4[:, :, MLA_NOPE_DIM:].reshape(MLA_KV_RANK, 512)], axis=1).astype(BF16)

    qm, km, vm, dq, dk, dv = _pre(x, mod, norm1_g[l][None], win, q_a_norm_g[l][None], wq,
                                  kv_a_norm_g[l][None], wkv, cos_t, sin_t)

    mla_out = _mla_attn(qm, km, vm)
    lambda_init = 0.8 - 0.6 * math.exp(-0.3 * l)
    lam_vecs = jnp.stack([lambda_q1[l], lambda_k1[l], lambda_q2[l], lambda_k2[l]]).astype(F32)
    d_out = _diff_attn(lam_vecs, subln_g[l][None], dq, dk, dv, lambda_init)

    wr = jnp.concatenate([w_router_expert[l].T, w_router_group[l].T,
                          jnp.zeros((ROUTER_ROWS - N_EXPERTS - N_GROUPS, d), F32)], axis=0)
    br = jnp.concatenate([b_router_expert[l], b_router_group[l],
                          jnp.zeros((ROUTER_ROWS - N_EXPERTS - N_GROUPS,), F32)])[:, None]
    x1, h2, ids, ws, cnt = _post(x, mla_out, d_out, mod, w_out[l].astype(BF16), norm2_g[l][None], wr, br)

    tmx = EXPERT_TM
    counts = cnt[:, 0].astype(jnp.int32)
    padded = ((counts + tmx - 1) // tmx) * tmx
    pends = jnp.cumsum(padded)
    pstarts = pends - padded
    rows = t * TOP_K + N_EXPERTS * tmx
    nb = rows // tmx
    block_expert = jnp.minimum(
        jnp.searchsorted(pends, jnp.arange(nb, dtype=jnp.int32) * tmx, side='right'),
        N_EXPERTS - 1).astype(jnp.int32)
    nb_used = (pends[-1:] // tmx).astype(jnp.int32)
    pad_start = (pstarts + counts).astype(jnp.int32)

    ids_t = ids.transpose(1, 0, 2).reshape(4, t)
    dest0 = (pstarts[ids_t[0]] + ids_t[2]).astype(jnp.int32)
    dest1 = (pstarts[ids_t[1]] + ids_t[3]).astype(jnp.int32)
    w_tok = ws.transpose(0, 2, 1).reshape(t, TOP_K)

    xs = _dispatch(dest0, dest1, pad_start, h2, rows)
    ys = _experts(block_expert, nb_used, xs, w_expert_gate[l].astype(BF16), w_expert_up[l].astype(BF16),
                  w_expert_down[l].astype(BF16))
    out = _combine(dest0, dest1, x1.reshape(t, d), w_tok, mod, final_norm_g, ys, s)
    return out.reshape(b, s, d)


def kernel(x, c, w_ada, b_ada, norm1_g, w_in, q_a_norm_g, w_q_up, kv_a_norm_g, w_kv_up, lambda_q1, lambda_k1,
           lambda_q2, lambda_k2, subln_g, w_out, norm2_g, w_router_group, b_router_group, w_router_expert,
           b_router_expert, w_expert_gate, w_expert_up, w_expert_down, final_norm_g):
    b, s, d = x.shape
    depth = w_ada.shape[0]
    assert depth == 1, "the final RMSNorm is fused into the last layer's combine step"
    cos_t, sin_t = _rope_tables(s)
    mod = _ada(c, w_ada[0], b_ada[0]).reshape(b, 6, d)
    return _layer(0, x, mod, norm1_g, w_in, q_a_norm_g, w_q_up, kv_a_norm_g, w_kv_up, lambda_q1, lambda_k1,
                  lambda_q2, lambda_k2, subln_g, w_out, norm2_g, w_router_group, b_router_group,
                  w_router_expert, b_router_expert, w_expert_gate, w_expert_up, w_expert_down,
                  final_norm_g[None], cos_t, sin_t)
```

```python
import functools
import math

import jax
import jax.numpy as jnp
from jax import lax
from jax.experimental import pallas as pl
from jax.experimental.pallas import tpu as pltpu

D_MODEL = 1024
ROPE_THETA = 10000.0
EPS = 1e-6

MLA_HEADS = 4
MLA_Q_RANK = 256
MLA_KV_RANK = 256
MLA_NOPE_DIM = 128
MLA_ROPE_DIM = 64
MLA_V_DIM = 128
MLA_QK_PAD = 256

DIFF_HEADS = 4
DIFF_QK_DIM = 64
DIFF_V_DIM = 128

N_GROUPS = 4
EXPERTS_PER_GROUP = 8
N_EXPERTS = 32
TOP_K = 2
D_EXPERT = 512

LANES = 128
ROW_CHUNKS = D_MODEL // LANES
ROUTER_ROWS = 40

VMEM_LIMIT = 56 * 1024 * 1024

PRE_TM = 512
ATT_TQ = 512
ATT_TK = 1024
ATT_UNROLL = 8
LOG2E = math.log2(math.e)
POST_TM = 512
DISPATCH_TM = 512
EXPERT_TM = 256
COMBINE_TM = 256

F32 = jnp.float32
BF16 = jnp.bfloat16
NEG_INF = float("-inf")


def _cparams(sem):
    return pltpu.CompilerParams(dimension_semantics=sem, vmem_limit_bytes=VMEM_LIMIT)


def _ada_kernel(c_ref, w_ref, b_ref, o_ref):
    c = c_ref[...]
    sc = c / (1.0 + jnp.exp(-c))
    o_ref[...] = jnp.dot(sc, w_ref[...], preferred_element_type=F32,
                         precision=lax.Precision.HIGHEST) + b_ref[...]


def _ada(c, w_ada, b_ada):
    b, d = c.shape
    n = w_ada.shape[1]
    tn = 1536
    return pl.pallas_call(
        _ada_kernel,
        out_shape=jax.ShapeDtypeStruct((b, n), F32),
        grid=(n // tn,),
        in_specs=[pl.BlockSpec((b, d), lambda j: (0, 0)),
                  pl.BlockSpec((d, tn), lambda j: (0, j)),
                  pl.BlockSpec((1, tn), lambda j: (0, j))],
        out_specs=pl.BlockSpec((b, tn), lambda j: (0, j)),
        compiler_params=_cparams(("parallel",)),
        name="ada",
    )(c, w_ada, b_ada.reshape(1, n))


def _rms(x, g):
    return x * lax.rsqrt(jnp.mean(x * x, axis=-1, keepdims=True) + EPS) * g


def _rope_tile(x, cos, sin_signed, first_half):
    fwd = pltpu.roll(x, 96, axis=1)
    bwd = pltpu.roll(x, 32, axis=1)
    return x * cos + jnp.where(first_half, fwd, bwd) * sin_signed


def _pre_kernel(x_ref, mod_ref, g1_ref, win_ref, gq_ref, wq_ref, gkv_ref, wkv_ref, cos_ref, sin_ref,
                qm_ref, km_ref, vm_ref, dq_ref, dk_ref, dv_ref):
    x = x_ref[0]
    shift1 = mod_ref[0, 0:1, :]
    scale1 = mod_ref[0, 1:2, :]
    h = _rms(x, g1_ref[...]) * (1.0 + scale1) + shift1
    z = jnp.dot(h.astype(BF16), win_ref[...], preferred_element_type=F32)

    cos = cos_ref[...]
    sin = sin_ref[...]
    lane = lax.broadcasted_iota(jnp.int32, cos.shape, 1)
    first_half = (lane % 64) < 32

    cq = _rms(z[:, 0:256], gq_ref[...])
    q = jnp.dot(cq.astype(BF16), wq_ref[...], preferred_element_type=F32)
    ckv = _rms(z[:, 256:512], gkv_ref[...])
    kv = jnp.dot(ckv.astype(BF16), wkv_ref[...], preferred_element_type=F32)
    kpe = _rope_tile(z[:, 2048:2176], cos, sin, first_half).astype(BF16)

    q_scale = LOG2E * (MLA_NOPE_DIM + MLA_ROPE_DIM) ** -0.5
    for hd in range(MLA_HEADS):
        qn = q[:, hd * 128:(hd + 1) * 128] * q_scale
        qp = _rope_tile(q[:, 512 + hd * 128:512 + (hd + 1) * 128], cos, sin, first_half) * q_scale
        qm_ref[0, hd, :, 0:128] = qn.astype(BF16)
        qm_ref[0, hd, :, 128:256] = qp.astype(BF16)
        km_ref[0, hd, :, 0:128] = kv[:, hd * 128:(hd + 1) * 128].astype(BF16)
        km_ref[0, hd, :, 128:256] = kpe
        vm_ref[0, hd] = kv[:, 512 + hd * 128:512 + (hd + 1) * 128].astype(BF16)

    d_scale = LOG2E * DIFF_QK_DIM ** -0.5
    for hd in range(DIFF_HEADS):
        dq = _rope_tile(z[:, 512 + hd * 128:512 + (hd + 1) * 128], cos, sin, first_half) * d_scale
        dk = _rope_tile(z[:, 1024 + hd * 128:1024 + (hd + 1) * 128], cos, sin, first_half)
        dq_ref[0, hd] = dq.astype(BF16)
        dk_ref[0, hd] = dk.astype(BF16)
        dv_ref[0, hd] = z[:, 1536 + hd * 128:1536 + (hd + 1) * 128].astype(BF16)


def _pre(x, mod, g1, win, gq, wq, gkv, wkv, cos_t, sin_t):
    b, s, d = x.shape
    tm = PRE_TM
    full = lambda a: pl.BlockSpec(a.shape, lambda bi, i: (0,) * a.ndim)
    head_out = lambda w: pl.BlockSpec((1, 4, tm, w), lambda bi, i: (bi, 0, i, 0))
    head_shape = lambda w: jax.ShapeDtypeStruct((b, 4, s, w), BF16)
    return pl.pallas_call(
        _pre_kernel,
        out_shape=(head_shape(256), head_shape(256), head_shape(128),
                   head_shape(128), head_shape(128), head_shape(128)),
        grid=(b, s // tm),
        in_specs=[pl.BlockSpec((1, tm, d), lambda bi, i: (bi, i, 0)),
                  pl.BlockSpec((1, 6, d), lambda bi, i: (bi, 0, 0)),
                  full(g1), full(win), full(gq), full(wq), full(gkv), full(wkv),
                  pl.BlockSpec((tm, LANES), lambda bi, i: (i, 0)),
                  pl.BlockSpec((tm, LANES), lambda bi, i: (i, 0))],
        out_specs=(head_out(256), head_out(256), head_out(128),
                   head_out(128), head_out(128), head_out(128)),
        compiler_params=_cparams(("parallel", "parallel")),
        name="pre",
    )(x, mod, g1, win, gq, wq, gkv, wkv, cos_t, sin_t)


def _scores(q, k_ref, chunk):
    off = pl.multiple_of(chunk * ATT_TK, ATT_TK)
    k = k_ref[0, 0, pl.ds(off, ATT_TK), :]
    return lax.dot_general(q, k, (((1,), (1,)), ((), ())), preferred_element_type=F32)


def _softmax_pv(s, vx, m, accx):
    m_new = jnp.maximum(m, jnp.max(s, axis=-1, keepdims=True))
    alpha = jnp.exp2(m - m_new)
    p = jnp.exp2(s - m_new)
    accx_new = alpha * accx + jnp.dot(p.astype(BF16), vx, preferred_element_type=F32)
    return m_new, accx_new


def _attend(q_parts, k_ref, v_ref, s_ref):
    n_chunks = k_ref.shape[2] // ATT_TK
    n_parts = len(q_parts)
    ones = jnp.ones((ATT_TK, LANES), BF16)

    def half_step(chunk, cur, carry):
        nxt = 1 - cur
        chunk_next = jnp.minimum(chunk + 1, n_chunks - 1)
        off = pl.multiple_of(chunk * ATT_TK, ATT_TK)
        vx = jnp.concatenate([v_ref[0, 0, pl.ds(off, ATT_TK), :], ones], axis=-1)
        out = []
        for i, (q, (m, accx)) in enumerate(zip(q_parts, carry)):
            s_ref[nxt * n_parts + i] = _scores(q, k_ref, chunk_next)
            out.append(_softmax_pv(s_ref[cur * n_parts + i], vx, m, accx))
        return tuple(out)

    def body(jj, carry):
        for u in range(ATT_UNROLL):
            carry = half_step(ATT_UNROLL * jj + u, u % 2, carry)
        return carry

    for i, q in enumerate(q_parts):
        s_ref[i] = _scores(q, k_ref, 0)
    init = tuple((jnp.full((q.shape[0], 1), NEG_INF, F32), jnp.zeros((q.shape[0], 2 * LANES), F32))
                 for q in q_parts)
    res = lax.fori_loop(0, n_chunks // ATT_UNROLL, body, init)
    return [accx for _, accx in res]


def _normalise(accx):
    return accx[:, :LANES] / accx[:, LANES:]


def _mla_kernel(q_ref, k_ref, v_ref, o_ref, s_ref):
    (accx,) = _attend([q_ref[0, 0]], k_ref, v_ref, s_ref)
    o_ref[0] = _normalise(accx).astype(o_ref.dtype)


def _mla_attn(q, k, v):
    b, h, s, _ = q.shape
    tq = ATT_TQ
    return pl.pallas_call(
        _mla_kernel,
        out_shape=jax.ShapeDtypeStruct((b, s, h * MLA_V_DIM), BF16),
        grid=(b, h, s // tq),
        in_specs=[pl.BlockSpec((1, 1, tq, MLA_QK_PAD), lambda bi, hi, i: (bi, hi, i, 0)),
                  pl.BlockSpec((1, 1, s, MLA_QK_PAD), lambda bi, hi, i: (bi, hi, 0, 0)),
                  pl.BlockSpec((1, 1, s, MLA_V_DIM), lambda bi, hi, i: (bi, hi, 0, 0))],
        out_specs=pl.BlockSpec((1, tq, MLA_V_DIM), lambda bi, hi, i: (bi, i, hi)),
        scratch_shapes=[pltpu.VMEM((2, tq, ATT_TK), F32)],
        compiler_params=_cparams(("parallel", "parallel", "parallel")),
        name="mla_attn",
    )(q, k, v)


def _diff_kernel(lam_ref, g_ref, q_ref, k_ref, v_ref, o_ref, s_ref, *, out_scale, lambda_init):
    q = q_ref[0, 0]
    lane = lax.broadcasted_iota(jnp.int32, q.shape, 1)
    zero = jnp.zeros_like(q)
    q1 = jnp.where(lane < DIFF_QK_DIM, q, zero)
    q2 = jnp.where(lane < DIFF_QK_DIM, zero, q)
    a1, a2 = _attend([q1, q2], k_ref, v_ref, s_ref)

    lv = lam_ref[...]
    lam = (jnp.exp(jnp.sum(lv[0:1] * lv[1:2], axis=-1, keepdims=True))
           - jnp.exp(jnp.sum(lv[2:3] * lv[3:4], axis=-1, keepdims=True)) + lambda_init)
    o = _normalise(a1) - lam * _normalise(a2)
    o_ref[0] = (_rms(o, g_ref[...]) * out_scale).astype(o_ref.dtype)


def _diff_attn(lam_vecs, subln_g, q, k, v, lambda_init):
    b, h, s, _ = q.shape
    tq = ATT_TQ
    kern = functools.partial(_diff_kernel, out_scale=1.0 - lambda_init, lambda_init=lambda_init)
    return pl.pallas_call(
        kern,
        out_shape=jax.ShapeDtypeStruct((b, s, h * DIFF_V_DIM), BF16),
        grid=(b, h, s // tq),
        in_specs=[pl.BlockSpec((4, DIFF_QK_DIM), lambda bi, hi, i: (0, 0)),
                  pl.BlockSpec((1, DIFF_V_DIM), lambda bi, hi, i: (0, 0)),
                  pl.BlockSpec((1, 1, tq, 128), lambda bi, hi, i: (bi, hi, i, 0)),
                  pl.BlockSpec((1, 1, s, 128), lambda bi, hi, i: (bi, hi, 0, 0)),
                  pl.BlockSpec((1, 1, s, DIFF_V_DIM), lambda bi, hi, i: (bi, hi, 0, 0))],
        out_specs=pl.BlockSpec((1, tq, DIFF_V_DIM), lambda bi, hi, i: (bi, i, hi)),
        scratch_shapes=[pltpu.VMEM((4, tq, ATT_TK), F32)],
        compiler_params=_cparams(("parallel", "parallel", "parallel")),
        name="diff_attn",
    )(lam_vecs, subln_g, q, k, v)


def _post_kernel(x_ref, mla_ref, dif_ref, mod_ref, wo_ref, g2_ref, wr_ref, br_ref,
                 x1_ref, h2_ref, ids_ref, ws_ref, cnt_ref, base_ref):
    first = jnp.logical_and(pl.program_id(0) == 0, pl.program_id(1) == 0)

    @pl.when(first)
    def _():
        base_ref[...] = jnp.zeros_like(base_ref)

    tm = x_ref.shape[1]
    gate1 = mod_ref[0, 2:3, :]
    shift2 = mod_ref[0, 3:4, :]
    scale2 = mod_ref[0, 4:5, :]
    mixin = jnp.concatenate([mla_ref[0], dif_ref[0]], axis=-1)
    mix = jnp.dot(mixin, wo_ref[...], preferred_element_type=F32)
    x1 = x_ref[0] + gate1 * mix
    x1_ref[0] = x1
    h2 = _rms(x1, g2_ref[...]) * (1.0 + scale2) + shift2
    for c in range(ROW_CHUNKS):
        h2_ref[:, c, :] = h2[:, c * LANES:(c + 1) * LANES]

    logits = lax.dot_general(wr_ref[...], h2, (((1,), (1,)), ((), ())), preferred_element_type=F32,
                             precision=lax.Precision.HIGHEST) + br_ref[...]
    el = logits[0:N_EXPERTS]
    gl = logits[N_EXPERTS:N_EXPERTS + N_GROUPS]

    grow = lax.broadcasted_iota(jnp.int32, gl.shape, 0).astype(F32)
    gmax = jnp.max(gl, axis=0, keepdims=True)
    gidx = jnp.min(jnp.where(gl == gmax, grow, float(N_GROUPS)), axis=0, keepdims=True)
    gprob = 1.0 / jnp.sum(jnp.exp(gl - gmax), axis=0, keepdims=True)

    erow_i = lax.broadcasted_iota(jnp.int32, el.shape, 0)
    erow = erow_i.astype(F32)
    egrp = (erow_i // EXPERTS_PER_GROUP).astype(F32)
    masked = jnp.where(egrp == gidx, el, NEG_INF)
    l1 = jnp.max(masked, axis=0, keepdims=True)
    i1 = jnp.min(jnp.where(masked == l1, erow, float(N_EXPERTS)), axis=0, keepdims=True)
    oh1 = erow == i1
    masked2 = jnp.where(oh1, NEG_INF, masked)
    l2 = jnp.max(masked2, axis=0, keepdims=True)
    i2 = jnp.min(jnp.where(masked2 == l2, erow, float(N_EXPERTS)), axis=0, keepdims=True)
    oh2 = erow == i2
    e21 = jnp.exp(l2 - l1)
    w1 = gprob * (1.0 / (1.0 + e21))
    w2 = gprob * (e21 / (1.0 + e21))

    oh = jnp.where(jnp.logical_or(oh1, oh2), 1.0, 0.0)
    tr = lax.broadcasted_iota(jnp.int32, (tm, tm), 0)
    tc = lax.broadcasted_iota(jnp.int32, (tm, tm), 1)
    upper = jnp.where(tr < tc, 1.0, 0.0).astype(BF16)
    pos = base_ref[...][:, 0:1] + jnp.dot(oh.astype(BF16), upper, preferred_element_type=F32)
    r1 = jnp.sum(jnp.where(oh1, pos, 0.0), axis=0, keepdims=True)
    r2 = jnp.sum(jnp.where(oh2, pos, 0.0), axis=0, keepdims=True)
    base_new = base_ref[...] + jnp.sum(oh, axis=1, keepdims=True)
    base_ref[...] = base_new
    cnt_ref[...] = base_new

    ids_ref[0] = jnp.concatenate([i1, i2, r1, r2], axis=0).astype(jnp.int32)
    ws_ref[0] = jnp.concatenate([w1, w2], axis=0)


def _post(x, mla_out, d_out, mod, wo, g2, wr, br):
    b, s, d = x.shape
    tm = POST_TM
    nt = s // tm
    full = lambda a: pl.BlockSpec(a.shape, lambda bi, i: (0,) * a.ndim)
    tok = lambda w: pl.BlockSpec((1, tm, w), lambda bi, i: (bi, i, 0))
    return pl.pallas_call(
        _post_kernel,
        out_shape=(jax.ShapeDtypeStruct((b, s, d), F32),
                   jax.ShapeDtypeStruct((b * s, ROW_CHUNKS, LANES), F32),
                   jax.ShapeDtypeStruct((b * nt, 4, tm), jnp.int32),
                   jax.ShapeDtypeStruct((b * nt, 2, tm), F32),
                   jax.ShapeDtypeStruct((N_EXPERTS, LANES), F32)),
        grid=(b, nt),
        in_specs=[tok(d), tok(mla_out.shape[2]), tok(d_out.shape[2]),
                  pl.BlockSpec((1, 6, d), lambda bi, i: (bi, 0, 0)),
                  full(wo), full(g2), full(wr), full(br)],
        out_specs=(tok(d),
                   pl.BlockSpec((tm, ROW_CHUNKS, LANES), lambda bi, i: (bi * nt + i, 0, 0)),
                   pl.BlockSpec((1, 4, tm), lambda bi, i: (bi * nt + i, 0, 0)),
                   pl.BlockSpec((1, 2, tm), lambda bi, i: (bi * nt + i, 0, 0)),
                   pl.BlockSpec((N_EXPERTS, LANES), lambda bi, i: (0, 0))),
        scratch_shapes=[pltpu.VMEM((N_EXPERTS, LANES), F32)],
        compiler_params=_cparams(("arbitrary", "arbitrary")),
        name="post",
    )(x, mla_out, d_out, mod, wo, g2, wr, br)


def _dispatch_kernel(d0_ref, d1_ref, pad_ref, h_ref, xs_ref, zero_ref, sem):
    tm = h_ref.shape[0]
    i = pl.program_id(0)

    @pl.when(i == 0)
    def _():
        zero_ref[...] = jnp.zeros_like(zero_ref)
        for e in range(N_EXPERTS):
            pltpu.make_async_copy(zero_ref, xs_ref.at[pl.ds(pad_ref[e], EXPERT_TM)], sem).start()
        for e in range(N_EXPERTS):
            pltpu.make_async_copy(zero_ref, xs_ref.at[pl.ds(pad_ref[e], EXPERT_TM)], sem).wait()
        n_rows = xs_ref.shape[0]
        for e in range(N_EXPERTS):
            start = pad_ref[N_EXPERTS] + e * EXPERT_TM

            @pl.when(start < n_rows)
            def _():
                pltpu.make_async_copy(zero_ref, xs_ref.at[pl.ds(start, EXPERT_TM)], sem).start()

        for e in range(N_EXPERTS):
            start = pad_ref[N_EXPERTS] + e * EXPERT_TM

            @pl.when(start < n_rows)
            def _():
                pltpu.make_async_copy(zero_ref, xs_ref.at[pl.ds(start, EXPERT_TM)], sem).wait()

    base = i * tm

    def issue(r, carry):
        pltpu.make_async_copy(h_ref.at[r], xs_ref.at[d0_ref[base + r]], sem).start()
        pltpu.make_async_copy(h_ref.at[r], xs_ref.at[d1_ref[base + r]], sem).start()
        return carry

    lax.fori_loop(0, tm, issue, 0)

    def drain(r, carry):
        pltpu.make_async_copy(h_ref.at[r], xs_ref.at[0], sem).wait()
        pltpu.make_async_copy(h_ref.at[r], xs_ref.at[0], sem).wait()
        return carry

    lax.fori_loop(0, tm, drain, 0)


def _dispatch(dest0, dest1, pad_start, h2, rows):
    t = h2.shape[0]
    tm = DISPATCH_TM
    return pl.pallas_call(
        _dispatch_kernel,
        out_shape=jax.ShapeDtypeStruct((rows, ROW_CHUNKS, LANES), F32),
        grid_spec=pltpu.PrefetchScalarGridSpec(
            num_scalar_prefetch=3,
            grid=(t // tm,),
            in_specs=[pl.BlockSpec((tm, ROW_CHUNKS, LANES), lambda i, *_: (i, 0, 0))],
            out_specs=pl.BlockSpec(memory_space=pl.ANY),
            scratch_shapes=[pltpu.VMEM((EXPERT_TM, ROW_CHUNKS, LANES), F32), pltpu.SemaphoreType.DMA]),
        compiler_params=_cparams(("arbitrary",)),
        name="dispatch",
    )(dest0, dest1, pad_start, h2)


def _expert_kernel(be_ref, nbu_ref, x_ref, wg_ref, wu_ref, wd_ref, y_ref):
    @pl.when(pl.program_id(0) < nbu_ref[0])
    def _():
        x = jnp.concatenate([x_ref[:, c, :] for c in range(ROW_CHUNKS)], axis=-1).astype(BF16)
        g = jnp.dot(x, wg_ref[0], preferred_element_type=F32)
        u = jnp.dot(x, wu_ref[0], preferred_element_type=F32)
        a = (g / (1.0 + jnp.exp(-g))) * u
        y = jnp.dot(a.astype(BF16), wd_ref[0], preferred_element_type=F32)
        for c in range(ROW_CHUNKS):
            y_ref[:, c, :] = y[:, c * LANES:(c + 1) * LANES]

    @pl.when(pl.program_id(0) >= nbu_ref[0])
    def _():
        y_ref[...] = jnp.zeros_like(y_ref)


def _experts(block_expert, nb_used, xs, wg, wu, wd):
    rows = xs.shape[0]
    d = D_MODEL
    tm = EXPERT_TM
    nb = rows // tm
    row_map = lambda j, be, nbu: (jnp.minimum(j, nbu[0] - 1), 0, 0)
    w_map = lambda j, be, nbu: (be[j], 0, 0)
    return pl.pallas_call(
        _expert_kernel,
        out_shape=jax.ShapeDtypeStruct((rows, ROW_CHUNKS, LANES), F32),
        grid_spec=pltpu.PrefetchScalarGridSpec(
            num_scalar_prefetch=2,
            grid=(nb,),
            in_specs=[pl.BlockSpec((tm, ROW_CHUNKS, LANES), row_map),
                      pl.BlockSpec((1, d, D_EXPERT), w_map),
                      pl.BlockSpec((1, d, D_EXPERT), w_map),
                      pl.BlockSpec((1, D_EXPERT, d), w_map)],
            out_specs=pl.BlockSpec((tm, ROW_CHUNKS, LANES), lambda j, be, nbu: (j, 0, 0))),
        compiler_params=_cparams(("arbitrary",)),
        name="experts",
    )(block_expert, nb_used, xs, wg, wu, wd)


def _combine_kernel(d0_ref, d1_ref, x1_ref, w_ref, mod_ref, gf_ref, ys_ref, o_ref, y0_ref, y1_ref, sem):
    tm = x1_ref.shape[0]
    base = pl.program_id(0) * tm

    def issue(r, carry):
        pltpu.make_async_copy(ys_ref.at[d0_ref[base + r]], y0_ref.at[r], sem).start()
        pltpu.make_async_copy(ys_ref.at[d1_ref[base + r]], y1_ref.at[r], sem).start()
        return carry

    lax.fori_loop(0, tm, issue, 0)

    def drain(r, carry):
        pltpu.make_async_copy(ys_ref.at[0], y0_ref.at[r], sem).wait()
        pltpu.make_async_copy(ys_ref.at[0], y1_ref.at[r], sem).wait()
        return carry

    lax.fori_loop(0, tm, drain, 0)

    gate2 = mod_ref[0, 5:6, :]
    w = w_ref[...]
    w0 = w[:, 0:1]
    w1 = w[:, 1:2]
    moe = jnp.concatenate([w0 * y0_ref[:, c, :] + w1 * y1_ref[:, c, :] for c in range(ROW_CHUNKS)], axis=-1)
    x2 = x1_ref[...] + gate2 * moe
    o_ref[...] = _rms(x2, gf_ref[...])


def _combine(dest0, dest1, x1, w_tok, mod, gf, ys, s_len):
    t, d = x1.shape
    tm = COMBINE_TM
    per_batch = s_len // tm
    return pl.pallas_call(
        _combine_kernel,
        out_shape=jax.ShapeDtypeStruct((t, d), F32),
        grid_spec=pltpu.PrefetchScalarGridSpec(
            num_scalar_prefetch=2,
            grid=(t // tm,),
            in_specs=[pl.BlockSpec((tm, d), lambda i, *_: (i, 0)),
                      pl.BlockSpec((tm, TOP_K), lambda i, *_: (i, 0)),
                      pl.BlockSpec((1, 6, d), lambda i, *_: (i // per_batch, 0, 0)),
                      pl.BlockSpec((1, d), lambda i, *_: (0, 0)),
                      pl.BlockSpec(memory_space=pl.ANY)],
            out_specs=pl.BlockSpec((tm, d), lambda i, *_: (i, 0)),
            scratch_shapes=[pltpu.VMEM((tm, ROW_CHUNKS, LANES), F32), pltpu.VMEM((tm, ROW_CHUNKS, LANES), F32),
                            pltpu.SemaphoreType.DMA]),
        compiler_params=_cparams(("arbitrary",)),
        name="combine",
    )(dest0, dest1, x1, w_tok, mod, gf, ys)


def _rope_tables(s_len):
    half = MLA_ROPE_DIM // 2
    inv = ROPE_THETA ** (-jnp.arange(0, MLA_ROPE_DIM, 2, dtype=F32) / MLA_ROPE_DIM)
    ang = jnp.arange(s_len, dtype=jnp.int32).astype(F32)[:, None] * inv[None, :]
    cos, sin = jnp.cos(ang), jnp.sin(ang)
    reps = LANES // MLA_ROPE_DIM
    cos_t = jnp.tile(jnp.concatenate([cos, cos], axis=1), (1, reps))
    sin_t = jnp.tile(jnp.concatenate([-sin, sin], axis=1), (1, reps))
    del half
    return cos_t, sin_t


def _layer(l, x, mod, norm1_g, w_in, q_a_norm_g, w_q_up, kv_a_norm_g, w_kv_up, lambda_q1, lambda_k1,
           lambda_q2, lambda_k2, subln_g, w_out, norm2_g, w_router_group, b_router_group, w_router_expert,
           b_router_expert, w_expert_gate, w_expert_up, w_expert_down, final_norm_g, cos_t, sin_t):
    b, s, d = x.shape
    t = b * s

    wi = w_in[l]
    win = jnp.concatenate([wi[:, 0:512], wi[:, 576:2112], wi[:, 512:576], jnp.zeros((d, 64), F32)],
                          axis=1).astype(BF16)
    wq4 = w_q_up[l].reshape(MLA_Q_RANK, MLA_HEADS, MLA_NOPE_DIM + MLA_ROPE_DIM)
    wq_pe = jnp.pad(wq4[:, :, MLA_NOPE_DIM:], ((0, 0), (0, 0), (0, 64)))
    wq = jnp.concatenate([wq4[:, :, :MLA_NOPE_DIM].reshape(MLA_Q_RANK, 512),
                          wq_pe.reshape(MLA_Q_RANK, 512)], axis=1).astype(BF16)
    wkv4 = w_kv_up[l].reshape(MLA_KV_RANK, MLA_HEADS, MLA_NOPE_DIM + MLA_V_DIM)
    wkv = jnp.concatenate([wkv4[:, :, :MLA_NOPE_DIM].reshape(MLA_KV_RANK, 512),
                           wkv4[:, :, MLA_NOPE_DIM:].reshape(MLA_KV_RANK, 512)], axis=1).astype(BF16)

    qm, km, vm, dq, dk, dv = _pre(x, mod, norm1_g[l][None], win, q_a_norm_g[l][None], wq,
                                  kv_a_norm_g[l][None], wkv, cos_t, sin_t)

    mla_out = _mla_attn(qm, km, vm)
    lambda_init = 0.8 - 0.6 * math.exp(-0.3 * l)
    lam_vecs = jnp.stack([lambda_q1[l], lambda_k1[l], lambda_q2[l], lambda_k2[l]]).astype(F32)
    d_out = _diff_attn(lam_vecs, subln_g[l][None], dq, dk, dv, lambda_init)

    wr = jnp.concatenate([w_router_expert[l].T, w_router_group[l].T,
                          jnp.zeros((ROUTER_ROWS - N_EXPERTS - N_GROUPS, d), F32)], axis=0)
    br = jnp.concatenate([b_router_expert[l], b_router_group[l],
                          jnp.zeros((ROUTER_ROWS - N_EXPERTS - N_GROUPS,), F32)])[:, None]
    x1, h2, ids, ws, cnt = _post(x, mla_out, d_out, mod, w_out[l].astype(BF16), norm2_g[l][None], wr, br)

    tmx = EXPERT_TM
    counts = cnt[:, 0].astype(jnp.int32)
    padded = ((counts + tmx - 1) // tmx) * tmx
    pends = jnp.cumsum(padded)
    pstarts = pends - padded
    rows = t * TOP_K + N_EXPERTS * tmx
    nb = rows // tmx
    block_row = jnp.arange(nb, dtype=jnp.int32) * tmx
    block_expert = jnp.minimum(jnp.sum((pends[None, :] <= block_row[:, None]).astype(jnp.int32), axis=1),
                               N_EXPERTS - 1).astype(jnp.int32)
    nb_used = (pends[-1:] // tmx).astype(jnp.int32)
    pad_start = jnp.concatenate([pstarts + counts, pends[-1:]]).astype(jnp.int32)

    ids_t = ids.transpose(1, 0, 2).reshape(4, t)
    dest0 = (pstarts[ids_t[0]] + ids_t[2]).astype(jnp.int32)
    dest1 = (pstarts[ids_t[1]] + ids_t[3]).astype(jnp.int32)
    w_tok = ws.transpose(0, 2, 1).reshape(t, TOP_K)

    xs = _dispatch(dest0, dest1, pad_start, h2, rows)
    ys = _experts(block_expert, nb_used, xs, w_expert_gate[l].astype(BF16), w_expert_up[l].astype(BF16),
                  w_expert_down[l].astype(BF16))
    out = _combine(dest0, dest1, x1.reshape(t, d), w_tok, mod, final_norm_g, ys, s)
    return out.reshape(b, s, d)


def kernel(x, c, w_ada, b_ada, norm1_g, w_in, q_a_norm_g, w_q_up, kv_a_norm_g, w_kv_up, lambda_q1, lambda_k1,
           lambda_q2, lambda_k2, subln_g, w_out, norm2_g, w_router_group, b_router_group, w_router_expert,
           b_router_expert, w_expert_gate, w_expert_up, w_expert_down, final_norm_g):
    b, s, d = x.shape
    depth = w_ada.shape[0]
    assert depth == 1, "the final RMSNorm is fused into the last layer's combine step"
    cos_t, sin_t = _rope_tables(s)
    mod = _ada(c, w_ada[0], b_ada[0]).reshape(b, 6, d)
    return _layer(0, x, mod, norm1_g, w_in, q_a_norm_g, w_q_up, kv_a_norm_g, w_kv_up, lambda_q1, lambda_k1,
                  lambda_q2, lambda_k2, subln_g, w_out, norm2_g, w_router_group, b_router_group,
                  w_router_expert, b_router_expert, w_expert_gate, w_expert_up, w_expert_down,
                  final_norm_g[None], cos_t, sin_t)
```

```python
import functools
import math

import jax
import jax.numpy as jnp
from jax import lax
from jax.experimental import pallas as pl
from jax.experimental.pallas import tpu as pltpu

D_MODEL = 1024
ROPE_THETA = 10000.0
EPS = 1e-6

MLA_HEADS = 4
MLA_Q_RANK = 256
MLA_KV_RANK = 256
MLA_NOPE_DIM = 128
MLA_ROPE_DIM = 64
MLA_V_DIM = 128
MLA_QK_PAD = 256

DIFF_HEADS = 4
DIFF_QK_DIM = 64
DIFF_V_DIM = 128

N_GROUPS = 4
EXPERTS_PER_GROUP = 8
N_EXPERTS = 32
TOP_K = 2
D_EXPERT = 512

LANES = 128
ROW_CHUNKS = D_MODEL // LANES
ROUTER_ROWS = 40

VMEM_LIMIT = 56 * 1024 * 1024

PRE_TM = 512
ATT_TQ = 512
ATT_TK = 1024
ATT_UNROLL = 8
LOG2E = math.log2(math.e)
POST_TM = 512
DISPATCH_TM = 512
EXPERT_TM = 512
COMBINE_TM = 256

F32 = jnp.float32
BF16 = jnp.bfloat16
NEG_INF = float("-inf")


def _cparams(sem):
    return pltpu.CompilerParams(dimension_semantics=sem, vmem_limit_bytes=VMEM_LIMIT)


def _ada_kernel(c_ref, w_ref, b_ref, o_ref):
    c = c_ref[...]
    sc = c / (1.0 + jnp.exp(-c))
    o_ref[...] = jnp.dot(sc, w_ref[...], preferred_element_type=F32,
                         precision=lax.Precision.HIGHEST) + b_ref[...]


def _ada(c, w_ada, b_ada):
    b, d = c.shape
    n = w_ada.shape[1]
    tn = 1536
    return pl.pallas_call(
        _ada_kernel,
        out_shape=jax.ShapeDtypeStruct((b, n), F32),
        grid=(n // tn,),
        in_specs=[pl.BlockSpec((b, d), lambda j: (0, 0)),
                  pl.BlockSpec((d, tn), lambda j: (0, j)),
                  pl.BlockSpec((1, tn), lambda j: (0, j))],
        out_specs=pl.BlockSpec((b, tn), lambda j: (0, j)),
        compiler_params=_cparams(("parallel",)),
        name="ada",
    )(c, w_ada, b_ada.reshape(1, n))


def _rms(x, g):
    return x * lax.rsqrt(jnp.mean(x * x, axis=-1, keepdims=True) + EPS) * g


def _token_tile(ref, t):
    return ref.at[pl.ds(pl.multiple_of(t * ROW_CHUNKS, ROW_CHUNKS), ROW_CHUNKS), :]


def _token_tiles(ref, t, n):
    return ref.at[pl.ds(pl.multiple_of(t * ROW_CHUNKS, ROW_CHUNKS), n * ROW_CHUNKS), :]


def _load_token_rows(ref):
    tm = ref.shape[0] // ROW_CHUNKS
    return [ref[pl.ds(c, tm, stride=ROW_CHUNKS), :] for c in range(ROW_CHUNKS)]


def _store_token_rows(ref, val):
    tm = ref.shape[0] // ROW_CHUNKS
    for c in range(ROW_CHUNKS):
        ref[pl.ds(c, tm, stride=ROW_CHUNKS), :] = val[:, c * LANES:(c + 1) * LANES]


def _rope_tile(x, cos, sin_signed, first_half):
    fwd = pltpu.roll(x, 96, axis=1)
    bwd = pltpu.roll(x, 32, axis=1)
    return x * cos + jnp.where(first_half, fwd, bwd) * sin_signed


def _pre_kernel(x_ref, mod_ref, g1_ref, win_ref, gq_ref, wq_ref, gkv_ref, wkv_ref, cos_ref, sin_ref,
                qm_ref, km_ref, vm_ref, dq_ref, dk_ref, dv_ref):
    x = x_ref[0]
    shift1 = mod_ref[0, 0:1, :]
    scale1 = mod_ref[0, 1:2, :]
    h = _rms(x, g1_ref[...]) * (1.0 + scale1) + shift1
    z = jnp.dot(h.astype(BF16), win_ref[...], preferred_element_type=F32)

    cos = cos_ref[...]
    sin = sin_ref[...]
    lane = lax.broadcasted_iota(jnp.int32, cos.shape, 1)
    first_half = (lane % 64) < 32

    cq = _rms(z[:, 0:256], gq_ref[...])
    q = jnp.dot(cq.astype(BF16), wq_ref[...], preferred_element_type=F32)
    ckv = _rms(z[:, 256:512], gkv_ref[...])
    kv = jnp.dot(ckv.astype(BF16), wkv_ref[...], preferred_element_type=F32)
    kpe = _rope_tile(z[:, 2048:2176], cos, sin, first_half).astype(BF16)

    q_scale = LOG2E * (MLA_NOPE_DIM + MLA_ROPE_DIM) ** -0.5
    for hd in range(MLA_HEADS):
        qn = q[:, hd * 128:(hd + 1) * 128] * q_scale
        qp = _rope_tile(q[:, 512 + hd * 128:512 + (hd + 1) * 128], cos, sin, first_half) * q_scale
        qm_ref[0, hd, :, 0:128] = qn.astype(BF16)
        qm_ref[0, hd, :, 128:256] = qp.astype(BF16)
        km_ref[0, hd, :, 0:128] = kv[:, hd * 128:(hd + 1) * 128].astype(BF16)
        km_ref[0, hd, :, 128:256] = kpe
        vm_ref[0, hd] = kv[:, 512 + hd * 128:512 + (hd + 1) * 128].astype(BF16)

    d_scale = LOG2E * DIFF_QK_DIM ** -0.5
    for hd in range(DIFF_HEADS):
        dq = _rope_tile(z[:, 512 + hd * 128:512 + (hd + 1) * 128], cos, sin, first_half) * d_scale
        dk = _rope_tile(z[:, 1024 + hd * 128:1024 + (hd + 1) * 128], cos, sin, first_half)
        dq_ref[0, hd] = dq.astype(BF16)
        dk_ref[0, hd] = dk.astype(BF16)
        dv_ref[0, hd] = z[:, 1536 + hd * 128:1536 + (hd + 1) * 128].astype(BF16)


def _pre(x, mod, g1, win, gq, wq, gkv, wkv, cos_t, sin_t):
    b, s, d = x.shape
    tm = PRE_TM
    full = lambda a: pl.BlockSpec(a.shape, lambda bi, i: (0,) * a.ndim)
    head_out = lambda w: pl.BlockSpec((1, 4, tm, w), lambda bi, i: (bi, 0, i, 0))
    head_shape = lambda w: jax.ShapeDtypeStruct((b, 4, s, w), BF16)
    return pl.pallas_call(
        _pre_kernel,
        out_shape=(head_shape(256), head_shape(256), head_shape(128),
                   head_shape(128), head_shape(128), head_shape(128)),
        grid=(b, s // tm),
        in_specs=[pl.BlockSpec((1, tm, d), lambda bi, i: (bi, i, 0)),
                  pl.BlockSpec((1, 6, d), lambda bi, i: (bi, 0, 0)),
                  full(g1), full(win), full(gq), full(wq), full(gkv), full(wkv),
                  pl.BlockSpec((tm, LANES), lambda bi, i: (i, 0)),
                  pl.BlockSpec((tm, LANES), lambda bi, i: (i, 0))],
        out_specs=(head_out(256), head_out(256), head_out(128),
                   head_out(128), head_out(128), head_out(128)),
        compiler_params=_cparams(("parallel", "parallel")),
        name="pre",
    )(x, mod, g1, win, gq, wq, gkv, wkv, cos_t, sin_t)


def _scores(q, k_ref, chunk):
    off = pl.multiple_of(chunk * ATT_TK, ATT_TK)
    k = k_ref[0, 0, pl.ds(off, ATT_TK), :]
    return lax.dot_general(q, k, (((1,), (1,)), ((), ())), preferred_element_type=F32)


def _softmax_pv(s, vx, m, accx):
    m_new = jnp.maximum(m, jnp.max(s, axis=-1, keepdims=True))
    alpha = jnp.exp2(m - m_new)
    p = jnp.exp2(s - m_new)
    accx_new = alpha * accx + jnp.dot(p.astype(BF16), vx, preferred_element_type=F32)
    return m_new, accx_new


def _attend(q_parts, k_ref, v_ref, s_ref):
    n_chunks = k_ref.shape[2] // ATT_TK
    n_parts = len(q_parts)
    ones = jnp.ones((ATT_TK, LANES), BF16)

    def half_step(chunk, cur, carry):
        nxt = 1 - cur
        chunk_next = jnp.minimum(chunk + 1, n_chunks - 1)
        off = pl.multiple_of(chunk * ATT_TK, ATT_TK)
        vx = jnp.concatenate([v_ref[0, 0, pl.ds(off, ATT_TK), :], ones], axis=-1)
        out = []
        for i, (q, (m, accx)) in enumerate(zip(q_parts, carry)):
            s_ref[nxt * n_parts + i] = _scores(q, k_ref, chunk_next)
            out.append(_softmax_pv(s_ref[cur * n_parts + i], vx, m, accx))
        return tuple(out)

    def body(jj, carry):
        for u in range(ATT_UNROLL):
            carry = half_step(ATT_UNROLL * jj + u, u % 2, carry)
        return carry

    for i, q in enumerate(q_parts):
        s_ref[i] = _scores(q, k_ref, 0)
    init = tuple((jnp.full((q.shape[0], 1), NEG_INF, F32), jnp.zeros((q.shape[0], 2 * LANES), F32))
                 for q in q_parts)
    res = lax.fori_loop(0, n_chunks // ATT_UNROLL, body, init)
    return [accx for _, accx in res]


def _normalise(accx):
    return accx[:, :LANES] / accx[:, LANES:]


def _mla_kernel(q_ref, k_ref, v_ref, o_ref, s_ref):
    (accx,) = _attend([q_ref[0, 0]], k_ref, v_ref, s_ref)
    o_ref[0] = _normalise(accx).astype(o_ref.dtype)


def _mla_attn(q, k, v):
    b, h, s, _ = q.shape
    tq = ATT_TQ
    return pl.pallas_call(
        _mla_kernel,
        out_shape=jax.ShapeDtypeStruct((b, s, h * MLA_V_DIM), BF16),
        grid=(b, h, s // tq),
        in_specs=[pl.BlockSpec((1, 1, tq, MLA_QK_PAD), lambda bi, hi, i: (bi, hi, i, 0)),
                  pl.BlockSpec((1, 1, s, MLA_QK_PAD), lambda bi, hi, i: (bi, hi, 0, 0)),
                  pl.BlockSpec((1, 1, s, MLA_V_DIM), lambda bi, hi, i: (bi, hi, 0, 0))],
        out_specs=pl.BlockSpec((1, tq, MLA_V_DIM), lambda bi, hi, i: (bi, i, hi)),
        scratch_shapes=[pltpu.VMEM((2, tq, ATT_TK), F32)],
        compiler_params=_cparams(("parallel", "parallel", "parallel")),
        name="mla_attn",
    )(q, k, v)


def _diff_kernel(lam_ref, g_ref, q_ref, k_ref, v_ref, o_ref, s_ref, *, out_scale, lambda_init):
    q = q_ref[0, 0]
    lane = lax.broadcasted_iota(jnp.int32, q.shape, 1)
    zero = jnp.zeros_like(q)
    q1 = jnp.where(lane < DIFF_QK_DIM, q, zero)
    q2 = jnp.where(lane < DIFF_QK_DIM, zero, q)
    a1, a2 = _attend([q1, q2], k_ref, v_ref, s_ref)

    lv = lam_ref[...]
    lam = (jnp.exp(jnp.sum(lv[0:1] * lv[1:2], axis=-1, keepdims=True))
           - jnp.exp(jnp.sum(lv[2:3] * lv[3:4], axis=-1, keepdims=True)) + lambda_init)
    o = _normalise(a1) - lam * _normalise(a2)
    o_ref[0] = (_rms(o, g_ref[...]) * out_scale).astype(o_ref.dtype)


def _diff_attn(lam_vecs, subln_g, q, k, v, lambda_init):
    b, h, s, _ = q.shape
    tq = ATT_TQ
    kern = functools.partial(_diff_kernel, out_scale=1.0 - lambda_init, lambda_init=lambda_init)
    return pl.pallas_call(
        kern,
        out_shape=jax.ShapeDtypeStruct((b, s, h * DIFF_V_DIM), BF16),
        grid=(b, h, s // tq),
        in_specs=[pl.BlockSpec((4, DIFF_QK_DIM), lambda bi, hi, i: (0, 0)),
                  pl.BlockSpec((1, DIFF_V_DIM), lambda bi, hi, i: (0, 0)),
                  pl.BlockSpec((1, 1, tq, 128), lambda bi, hi, i: (bi, hi, i, 0)),
                  pl.BlockSpec((1, 1, s, 128), lambda bi, hi, i: (bi, hi, 0, 0)),
                  pl.BlockSpec((1, 1, s, DIFF_V_DIM), lambda bi, hi, i: (bi, hi, 0, 0))],
        out_specs=pl.BlockSpec((1, tq, DIFF_V_DIM), lambda bi, hi, i: (bi, i, hi)),
        scratch_shapes=[pltpu.VMEM((4, tq, ATT_TK), F32)],
        compiler_params=_cparams(("parallel", "parallel", "parallel")),
        name="diff_attn",
    )(lam_vecs, subln_g, q, k, v)


def _post_kernel(x_ref, mla_ref, dif_ref, mod_ref, wo_ref, g2_ref, wr_ref, br_ref,
                 x1_ref, h2_ref, ids_ref, ws_ref, cnt_ref, base_ref):
    first = jnp.logical_and(pl.program_id(0) == 0, pl.program_id(1) == 0)

    @pl.when(first)
    def _():
        base_ref[...] = jnp.zeros_like(base_ref)

    tm = x_ref.shape[1]
    gate1 = mod_ref[0, 2:3, :]
    shift2 = mod_ref[0, 3:4, :]
    scale2 = mod_ref[0, 4:5, :]
    mixin = jnp.concatenate([mla_ref[0], dif_ref[0]], axis=-1)
    mix = jnp.dot(mixin, wo_ref[...], preferred_element_type=F32)
    x1 = x_ref[0] + gate1 * mix
    x1_ref[0] = x1
    h2 = _rms(x1, g2_ref[...]) * (1.0 + scale2) + shift2
    _store_token_rows(h2_ref, h2)

    logits = lax.dot_general(wr_ref[...], h2, (((1,), (1,)), ((), ())), preferred_element_type=F32,
                             precision=lax.Precision.HIGHEST) + br_ref[...]
    el = logits[0:N_EXPERTS]
    gl = logits[N_EXPERTS:N_EXPERTS + N_GROUPS]

    grow = lax.broadcasted_iota(jnp.int32, gl.shape, 0).astype(F32)
    gmax = jnp.max(gl, axis=0, keepdims=True)
    gidx = jnp.min(jnp.where(gl == gmax, grow, float(N_GROUPS)), axis=0, keepdims=True)
    gprob = 1.0 / jnp.sum(jnp.exp(gl - gmax), axis=0, keepdims=True)

    erow_i = lax.broadcasted_iota(jnp.int32, el.shape, 0)
    erow = erow_i.astype(F32)
    egrp = (erow_i // EXPERTS_PER_GROUP).astype(F32)
    masked = jnp.where(egrp == gidx, el, NEG_INF)
    l1 = jnp.max(masked, axis=0, keepdims=True)
    i1 = jnp.min(jnp.where(masked == l1, erow, float(N_EXPERTS)), axis=0, keepdims=True)
    oh1 = erow == i1
    masked2 = jnp.where(oh1, NEG_INF, masked)
    l2 = jnp.max(masked2, axis=0, keepdims=True)
    i2 = jnp.min(jnp.where(masked2 == l2, erow, float(N_EXPERTS)), axis=0, keepdims=True)
    oh2 = erow == i2
    e21 = jnp.exp(l2 - l1)
    w1 = gprob * (1.0 / (1.0 + e21))
    w2 = gprob * (e21 / (1.0 + e21))

    oh = jnp.where(jnp.logical_or(oh1, oh2), 1.0, 0.0)
    tr = lax.broadcasted_iota(jnp.int32, (tm, tm), 0)
    tc = lax.broadcasted_iota(jnp.int32, (tm, tm), 1)
    upper = jnp.where(tr < tc, 1.0, 0.0).astype(BF16)
    pos = base_ref[...][:, 0:1] + jnp.dot(oh.astype(BF16), upper, preferred_element_type=F32)
    r1 = jnp.sum(jnp.where(oh1, pos, 0.0), axis=0, keepdims=True)
    r2 = jnp.sum(jnp.where(oh2, pos, 0.0), axis=0, keepdims=True)
    base_new = base_ref[...] + jnp.sum(oh, axis=1, keepdims=True)
    base_ref[...] = base_new
    cnt_ref[...] = base_new

    ids_ref[0] = jnp.concatenate([i1, i2, r1, r2], axis=0).astype(jnp.int32)
    ws_ref[0] = jnp.concatenate([w1, w2], axis=0)


def _post(x, mla_out, d_out, mod, wo, g2, wr, br):
    b, s, d = x.shape
    tm = POST_TM
    nt = s // tm
    full = lambda a: pl.BlockSpec(a.shape, lambda bi, i: (0,) * a.ndim)
    tok = lambda w: pl.BlockSpec((1, tm, w), lambda bi, i: (bi, i, 0))
    return pl.pallas_call(
        _post_kernel,
        out_shape=(jax.ShapeDtypeStruct((b, s, d), F32),
                   jax.ShapeDtypeStruct((b * s * ROW_CHUNKS, LANES), F32),
                   jax.ShapeDtypeStruct((b * nt, 4, tm), jnp.int32),
                   jax.ShapeDtypeStruct((b * nt, 2, tm), F32),
                   jax.ShapeDtypeStruct((N_EXPERTS, LANES), F32)),
        grid=(b, nt),
        in_specs=[tok(d), tok(mla_out.shape[2]), tok(d_out.shape[2]),
                  pl.BlockSpec((1, 6, d), lambda bi, i: (bi, 0, 0)),
                  full(wo), full(g2), full(wr), full(br)],
        out_specs=(tok(d),
                   pl.BlockSpec((tm * ROW_CHUNKS, LANES), lambda bi, i: (bi * nt + i, 0)),
                   pl.BlockSpec((1, 4, tm), lambda bi, i: (bi * nt + i, 0, 0)),
                   pl.BlockSpec((1, 2, tm), lambda bi, i: (bi * nt + i, 0, 0)),
                   pl.BlockSpec((N_EXPERTS, LANES), lambda bi, i: (0, 0))),
        scratch_shapes=[pltpu.VMEM((N_EXPERTS, LANES), F32)],
        compiler_params=_cparams(("arbitrary", "arbitrary")),
        name="post",
    )(x, mla_out, d_out, mod, wo, g2, wr, br)


def _dispatch_kernel(d0_ref, d1_ref, pad_ref, h_ref, xs_ref, zero_ref, sem):
    tm = h_ref.shape[0] // ROW_CHUNKS
    i = pl.program_id(0)

    @pl.when(i == 0)
    def _():
        zero_ref[...] = jnp.zeros_like(zero_ref)
        for e in range(N_EXPERTS):
            pltpu.make_async_copy(zero_ref, _token_tiles(xs_ref, pad_ref[e], EXPERT_TM), sem).start()
        for e in range(N_EXPERTS):
            pltpu.make_async_copy(zero_ref, _token_tiles(xs_ref, pad_ref[e], EXPERT_TM), sem).wait()
        n_rows = xs_ref.shape[0] // ROW_CHUNKS
        for e in range(N_EXPERTS):
            start = pad_ref[N_EXPERTS] + e * EXPERT_TM

            @pl.when(start < n_rows)
            def _():
                pltpu.make_async_copy(zero_ref, _token_tiles(xs_ref, start, EXPERT_TM), sem).start()

        for e in range(N_EXPERTS):
            start = pad_ref[N_EXPERTS] + e * EXPERT_TM

            @pl.when(start < n_rows)
            def _():
                pltpu.make_async_copy(zero_ref, _token_tiles(xs_ref, start, EXPERT_TM), sem).wait()

    base = i * tm

    def issue(r, carry):
        src = _token_tile(h_ref, r)
        pltpu.make_async_copy(src, _token_tile(xs_ref, d0_ref[base + r]), sem).start()
        pltpu.make_async_copy(src, _token_tile(xs_ref, d1_ref[base + r]), sem).start()
        return carry

    lax.fori_loop(0, tm, issue, 0, unroll=8)

    def drain(r, carry):
        src = _token_tile(h_ref, r)
        pltpu.make_async_copy(src, _token_tile(xs_ref, 0), sem).wait()
        pltpu.make_async_copy(src, _token_tile(xs_ref, 0), sem).wait()
        return carry

    lax.fori_loop(0, tm, drain, 0, unroll=8)


def _dispatch(dest0, dest1, pad_start, h2, rows):
    t = h2.shape[0] // ROW_CHUNKS
    tm = DISPATCH_TM
    return pl.pallas_call(
        _dispatch_kernel,
        out_shape=jax.ShapeDtypeStruct((rows * ROW_CHUNKS, LANES), F32),
        grid_spec=pltpu.PrefetchScalarGridSpec(
            num_scalar_prefetch=3,
            grid=(t // tm,),
            in_specs=[pl.BlockSpec((tm * ROW_CHUNKS, LANES), lambda i, *_: (i, 0))],
            out_specs=pl.BlockSpec(memory_space=pl.ANY),
            scratch_shapes=[pltpu.VMEM((EXPERT_TM * ROW_CHUNKS, LANES), F32), pltpu.SemaphoreType.DMA]),
        compiler_params=_cparams(("arbitrary",)),
        name="dispatch",
    )(dest0, dest1, pad_start, h2)


def _expert_kernel(be_ref, nbu_ref, x_ref, wg_ref, wu_ref, wd_ref, y_ref):
    @pl.when(pl.program_id(0) < nbu_ref[0])
    def _():
        x = jnp.concatenate(_load_token_rows(x_ref), axis=-1).astype(BF16)
        g = jnp.dot(x, wg_ref[0], preferred_element_type=F32)
        u = jnp.dot(x, wu_ref[0], preferred_element_type=F32)
        a = (g / (1.0 + jnp.exp(-g))) * u
        y = jnp.dot(a.astype(BF16), wd_ref[0], preferred_element_type=F32)
        _store_token_rows(y_ref, y)

    @pl.when(pl.program_id(0) >= nbu_ref[0])
    def _():
        y_ref[...] = jnp.zeros_like(y_ref)


def _experts(block_expert, nb_used, xs, wg, wu, wd):
    rows = xs.shape[0] // ROW_CHUNKS
    d = D_MODEL
    tm = EXPERT_TM
    nb = rows // tm
    row_map = lambda j, be, nbu: (jnp.minimum(j, nbu[0] - 1), 0)
    w_map = lambda j, be, nbu: (be[j], 0, 0)
    return pl.pallas_call(
        _expert_kernel,
        out_shape=jax.ShapeDtypeStruct((rows * ROW_CHUNKS, LANES), F32),
        grid_spec=pltpu.PrefetchScalarGridSpec(
            num_scalar_prefetch=2,
            grid=(nb,),
            in_specs=[pl.BlockSpec((tm * ROW_CHUNKS, LANES), row_map),
                      pl.BlockSpec((1, d, D_EXPERT), w_map),
                      pl.BlockSpec((1, d, D_EXPERT), w_map),
                      pl.BlockSpec((1, D_EXPERT, d), w_map)],
            out_specs=pl.BlockSpec((tm * ROW_CHUNKS, LANES), lambda j, be, nbu: (j, 0))),
        compiler_params=_cparams(("arbitrary",)),
        name="experts",
    )(block_expert, nb_used, xs, wg, wu, wd)


def _combine_kernel(d0_ref, d1_ref, x1_ref, w_ref, mod_ref, gf_ref, ys_ref, o_ref, y0_ref, y1_ref, sems):
    tm = x1_ref.shape[0]
    i = pl.program_id(0)
    n = pl.num_programs(0)

    def gather(step, slot, start):
        base = step * tm

        def one(r, carry):
            c0 = pltpu.make_async_copy(_token_tile(ys_ref, d0_ref[base + r] if start else 0),
                                       _token_tile(y0_ref.at[slot], r), sems.at[slot])
            c1 = pltpu.make_async_copy(_token_tile(ys_ref, d1_ref[base + r] if start else 0),
                                       _token_tile(y1_ref.at[slot], r), sems.at[slot])
            if start:
                c0.start()
                c1.start()
            else:
                c0.wait()
                c1.wait()
            return carry

        lax.fori_loop(0, tm, one, 0, unroll=8)

    slot = i % 2

    @pl.when(i == 0)
    def _():
        gather(0, 0, True)

    @pl.when(i + 1 < n)
    def _():
        gather(i + 1, 1 - slot, True)

    gather(i, slot, False)

    gate2 = mod_ref[0, 5:6, :]
    w = w_ref[...]
    w0 = w[:, 0:1]
    w1 = w[:, 1:2]
    y0 = _load_token_rows(y0_ref.at[slot])
    y1 = _load_token_rows(y1_ref.at[slot])
    moe = jnp.concatenate([w0 * a + w1 * b for a, b in zip(y0, y1)], axis=-1)
    x2 = x1_ref[...] + gate2 * moe
    o_ref[...] = _rms(x2, gf_ref[...])


def _combine(dest0, dest1, x1, w_tok, mod, gf, ys, s_len):
    t, d = x1.shape
    tm = COMBINE_TM
    per_batch = s_len // tm
    return pl.pallas_call(
        _combine_kernel,
        out_shape=jax.ShapeDtypeStruct((t, d), F32),
        grid_spec=pltpu.PrefetchScalarGridSpec(
            num_scalar_prefetch=2,
            grid=(t // tm,),
            in_specs=[pl.BlockSpec((tm, d), lambda i, *_: (i, 0)),
                      pl.BlockSpec((tm, TOP_K), lambda i, *_: (i, 0)),
                      pl.BlockSpec((1, 6, d), lambda i, *_: (i // per_batch, 0, 0)),
                      pl.BlockSpec((1, d), lambda i, *_: (0, 0)),
                      pl.BlockSpec(memory_space=pl.ANY)],
            out_specs=pl.BlockSpec((tm, d), lambda i, *_: (i, 0)),
            scratch_shapes=[pltpu.VMEM((2, tm * ROW_CHUNKS, LANES), F32),
                            pltpu.VMEM((2, tm * ROW_CHUNKS, LANES), F32),
                            pltpu.SemaphoreType.DMA((2,))]),
        compiler_params=_cparams(("arbitrary",)),
        name="combine",
    )(dest0, dest1, x1, w_tok, mod, gf, ys)


def _rope_tables(s_len):
    half = MLA_ROPE_DIM // 2
    inv = ROPE_THETA ** (-jnp.arange(0, MLA_ROPE_DIM, 2, dtype=F32) / MLA_ROPE_DIM)
    ang = jnp.arange(s_len, dtype=jnp.int32).astype(F32)[:, None] * inv[None, :]
    cos, sin = jnp.cos(ang), jnp.sin(ang)
    reps = LANES // MLA_ROPE_DIM
    cos_t = jnp.tile(jnp.concatenate([cos, cos], axis=1), (1, reps))
    sin_t = jnp.tile(jnp.concatenate([-sin, sin], axis=1), (1, reps))
    del half
    return cos_t, sin_t


def _layer(l, x, mod, norm1_g, w_in, q_a_norm_g, w_q_up, kv_a_norm_g, w_kv_up, lambda_q1, lambda_k1,
           lambda_q2, lambda_k2, subln_g, w_out, norm2_g, w_router_group, b_router_group, w_router_expert,
           b_router_expert, w_expert_gate, w_expert_up, w_expert_down, final_norm_g, cos_t, sin_t):
    b, s, d = x.shape
    t = b * s

    wi = w_in[l]
    win = jnp.concatenate([wi[:, 0:512], wi[:, 576:2112], wi[:, 512:576], jnp.zeros((d, 64), F32)],
                          axis=1).astype(BF16)
    wq4 = w_q_up[l].reshape(MLA_Q_RANK, MLA_HEADS, MLA_NOPE_DIM + MLA_ROPE_DIM)
    wq_pe = jnp.pad(wq4[:, :, MLA_NOPE_DIM:], ((0, 0), (0, 0), (0, 64)))
    wq = jnp.concatenate([wq4[:, :, :MLA_NOPE_DIM].reshape(MLA_Q_RANK, 512),
                          wq_pe.reshape(MLA_Q_RANK, 512)], axis=1).astype(BF16)
    wkv4 = w_kv_up[l].reshape(MLA_KV_RANK, MLA_HEADS, MLA_NOPE_DIM + MLA_V_DIM)
    wkv = jnp.concatenate([wkv4[:, :, :MLA_NOPE_DIM].reshape(MLA_KV_RANK, 512),
                           wkv4[:, :, MLA_NOPE_DIM:].reshape(MLA_KV_RANK, 512)], axis=1).astype(BF16)

    qm, km, vm, dq, dk, dv = _pre(x, mod, norm1_g[l][None], win, q_a_norm_g[l][None], wq,
                                  kv_a_norm_g[l][None], wkv, cos_t, sin_t)

    mla_out = _mla_attn(qm, km, vm)
    lambda_init = 0.8 - 0.6 * math.exp(-0.3 * l)
    lam_vecs = jnp.stack([lambda_q1[l], lambda_k1[l], lambda_q2[l], lambda_k2[l]]).astype(F32)
    d_out = _diff_attn(lam_vecs, subln_g[l][None], dq, dk, dv, lambda_init)

    wr = jnp.concatenate([w_router_expert[l].T, w_router_group[l].T,
                          jnp.zeros((ROUTER_ROWS - N_EXPERTS - N_GROUPS, d), F32)], axis=0)
    br = jnp.concatenate([b_router_expert[l], b_router_group[l],
                          jnp.zeros((ROUTER_ROWS - N_EXPERTS - N_GROUPS,), F32)])[:, None]
    x1, h2, ids, ws, cnt = _post(x, mla_out, d_out, mod, w_out[l].astype(BF16), norm2_g[l][None], wr, br)

    tmx = EXPERT_TM
    counts = cnt[:, 0].astype(jnp.int32)
    padded = ((counts + tmx - 1) // tmx) * tmx
    pends = jnp.cumsum(padded)
    pstarts = pends - padded
    rows = t * TOP_K + N_EXPERTS * tmx
    nb = rows // tmx
    block_row = jnp.arange(nb, dtype=jnp.int32) * tmx
    block_expert = jnp.minimum(jnp.sum((pends[None, :] <= block_row[:, None]).astype(jnp.int32), axis=1),
                               N_EXPERTS - 1).astype(jnp.int32)
    nb_used = (pends[-1:] // tmx).astype(jnp.int32)
    pad_start = jnp.concatenate([pstarts + counts, pends[-1:]]).astype(jnp.int32)

    ids_t = ids.transpose(1, 0, 2).reshape(4, t)
    dest0 = (pstarts[ids_t[0]] + ids_t[2]).astype(jnp.int32)
    dest1 = (pstarts[ids_t[1]] + ids_t[3]).astype(jnp.int32)
    w_tok = ws.transpose(0, 2, 1).reshape(t, TOP_K)

    xs = _dispatch(dest0, dest1, pad_start, h2, rows)
    ys = _experts(block_expert, nb_used, xs, w_expert_gate[l].astype(BF16), w_expert_up[l].astype(BF16),
                  w_expert_down[l].astype(BF16))
    out = _combine(dest0, dest1, x1.reshape(t, d), w_tok, mod, final_norm_g, ys, s)
    return out.reshape(b, s, d)


def kernel(x, c, w_ada, b_ada, norm1_g, w_in, q_a_norm_g, w_q_up, kv_a_norm_g, w_kv_up, lambda_q1, lambda_k1,
           lambda_q2, lambda_k2, subln_g, w_out, norm2_g, w_router_group, b_router_group, w_router_expert,
           b_router_expert, w_expert_gate, w_expert_up, w_expert_down, final_norm_g):
    b, s, d = x.shape
    depth = w_ada.shape[0]
    assert depth == 1, "the final RMSNorm is fused into the last layer's combine step"
    cos_t, sin_t = _rope_tables(s)
    mod = _ada(c, w_ada[0], b_ada[0]).reshape(b, 6, d)
    return _layer(0, x, mod, norm1_g, w_in, q_a_norm_g, w_q_up, kv_a_norm_g, w_kv_up, lambda_q1, lambda_k1,
                  lambda_q2, lambda_k2, subln_g, w_out, norm2_g, w_router_group, b_router_group,
                  w_router_expert, b_router_expert, w_expert_gate, w_expert_up, w_expert_down,
                  final_norm_g[None], cos_t, sin_t)
```

```python
import functools
import math

import jax
import jax.numpy as jnp
from jax import lax
from jax.experimental import pallas as pl
from jax.experimental.pallas import tpu as pltpu

D_MODEL = 1024
ROPE_THETA = 10000.0
EPS = 1e-6

MLA_HEADS = 4
MLA_Q_RANK = 256
MLA_KV_RANK = 256
MLA_NOPE_DIM = 128
MLA_ROPE_DIM = 64
MLA_V_DIM = 128
MLA_QK_PAD = 256

DIFF_HEADS = 4
DIFF_QK_DIM = 64
DIFF_V_DIM = 128

N_GROUPS = 4
EXPERTS_PER_GROUP = 8
N_EXPERTS = 32
TOP_K = 2
D_EXPERT = 512

LANES = 128
ROW_CHUNKS = D_MODEL // LANES
ROUTER_ROWS = 40

VMEM_LIMIT = 56 * 1024 * 1024

PRE_TM = 512
ATT_TQ = 512
ATT_TK = 1024
ATT_SLOTS = 3
SUBLANES = 8
MAX_CHAINS = 2
ONES_ROWS = 16
LOG2E = math.log2(math.e)
POST_TM = 512
DISPATCH_TM = 512
EXPERT_TM = 512
COMBINE_TM = 256

F32 = jnp.float32
BF16 = jnp.bfloat16
NEG_INF = float("-inf")


def _cparams(sem):
    return pltpu.CompilerParams(dimension_semantics=sem, vmem_limit_bytes=VMEM_LIMIT)


def _ada_kernel(c_ref, w_ref, b_ref, o_ref):
    c = c_ref[...]
    sc = c / (1.0 + jnp.exp(-c))
    o_ref[...] = jnp.dot(sc, w_ref[...], preferred_element_type=F32,
                         precision=lax.Precision.HIGHEST) + b_ref[...]


def _ada(c, w_ada, b_ada):
    b, d = c.shape
    n = w_ada.shape[1]
    tn = 1536
    return pl.pallas_call(
        _ada_kernel,
        out_shape=jax.ShapeDtypeStruct((b, n), F32),
        grid=(n // tn,),
        in_specs=[pl.BlockSpec((b, d), lambda j: (0, 0)),
                  pl.BlockSpec((d, tn), lambda j: (0, j)),
                  pl.BlockSpec((1, tn), lambda j: (0, j))],
        out_specs=pl.BlockSpec((b, tn), lambda j: (0, j)),
        compiler_params=_cparams(("parallel",)),
        name="ada",
    )(c, w_ada, b_ada.reshape(1, n))


def _rms(x, g):
    return x * lax.rsqrt(jnp.mean(x * x, axis=-1, keepdims=True) + EPS) * g


def _token_tile(ref, t):
    return ref.at[pl.ds(pl.multiple_of(t * ROW_CHUNKS, ROW_CHUNKS), ROW_CHUNKS), :]


def _token_tiles(ref, t, n):
    return ref.at[pl.ds(pl.multiple_of(t * ROW_CHUNKS, ROW_CHUNKS), n * ROW_CHUNKS), :]


def _load_token_rows(ref):
    tm = ref.shape[0] // ROW_CHUNKS
    return [ref[pl.ds(c, tm, stride=ROW_CHUNKS), :] for c in range(ROW_CHUNKS)]


def _store_token_rows(ref, val):
    tm = ref.shape[0] // ROW_CHUNKS
    for c in range(ROW_CHUNKS):
        ref[pl.ds(c, tm, stride=ROW_CHUNKS), :] = val[:, c * LANES:(c + 1) * LANES]


def _rope_tile(x, cos, sin_signed, first_half):
    fwd = pltpu.roll(x, 96, axis=1)
    bwd = pltpu.roll(x, 32, axis=1)
    return x * cos + jnp.where(first_half, fwd, bwd) * sin_signed


def _pre_kernel(x_ref, mod_ref, g1_ref, win_ref, gq_ref, wq_ref, gkv_ref, wkv_ref, cos_ref, sin_ref,
                qm_ref, km_ref, vm_ref, dq_ref, dk_ref, dv_ref):
    x = x_ref[0]
    shift1 = mod_ref[0, 0:1, :]
    scale1 = mod_ref[0, 1:2, :]
    h = _rms(x, g1_ref[...]) * (1.0 + scale1) + shift1
    z = jnp.dot(h.astype(BF16), win_ref[...], preferred_element_type=F32)

    cos = cos_ref[...]
    sin = sin_ref[...]
    lane = lax.broadcasted_iota(jnp.int32, cos.shape, 1)
    first_half = (lane % 64) < 32

    cq = _rms(z[:, 0:256], gq_ref[...])
    q = jnp.dot(cq.astype(BF16), wq_ref[...], preferred_element_type=F32)
    ckv = _rms(z[:, 256:512], gkv_ref[...])
    kv = jnp.dot(ckv.astype(BF16), wkv_ref[...], preferred_element_type=F32)
    kpe = _rope_tile(z[:, 2048:2176], cos, sin, first_half).astype(BF16)

    q_scale = LOG2E * (MLA_NOPE_DIM + MLA_ROPE_DIM) ** -0.5
    for hd in range(MLA_HEADS):
        qn = q[:, hd * 128:(hd + 1) * 128] * q_scale
        qp = _rope_tile(q[:, 512 + hd * 128:512 + (hd + 1) * 128], cos, sin, first_half) * q_scale
        qm_ref[0, hd, :, 0:128] = qn.astype(BF16)
        qm_ref[0, hd, :, 128:256] = qp.astype(BF16)
        km_ref[0, hd, :, 0:128] = kv[:, hd * 128:(hd + 1) * 128].astype(BF16)
        km_ref[0, hd, :, 128:256] = kpe
        vm_ref[0, hd] = kv[:, 512 + hd * 128:512 + (hd + 1) * 128].T.astype(BF16)

    d_scale = LOG2E * DIFF_QK_DIM ** -0.5
    for hd in range(DIFF_HEADS):
        dq = _rope_tile(z[:, 512 + hd * 128:512 + (hd + 1) * 128], cos, sin, first_half) * d_scale
        dk = _rope_tile(z[:, 1024 + hd * 128:1024 + (hd + 1) * 128], cos, sin, first_half)
        dq_ref[0, hd] = dq.astype(BF16)
        dk_ref[0, hd] = dk.astype(BF16)
        dv_ref[0, hd] = z[:, 1536 + hd * 128:1536 + (hd + 1) * 128].T.astype(BF16)


def _pre(x, mod, g1, win, gq, wq, gkv, wkv, cos_t, sin_t):
    b, s, d = x.shape
    tm = PRE_TM
    full = lambda a: pl.BlockSpec(a.shape, lambda bi, i: (0,) * a.ndim)
    head_out = lambda w: pl.BlockSpec((1, 4, tm, w), lambda bi, i: (bi, 0, i, 0))
    head_shape = lambda w: jax.ShapeDtypeStruct((b, 4, s, w), BF16)
    vt_out = pl.BlockSpec((1, 4, 128, tm), lambda bi, i: (bi, 0, 0, i))
    vt_shape = jax.ShapeDtypeStruct((b, 4, 128, s), BF16)
    return pl.pallas_call(
        _pre_kernel,
        out_shape=(head_shape(256), head_shape(256), vt_shape,
                   head_shape(128), head_shape(128), vt_shape),
        grid=(b, s // tm),
        in_specs=[pl.BlockSpec((1, tm, d), lambda bi, i: (bi, i, 0)),
                  pl.BlockSpec((1, 6, d), lambda bi, i: (bi, 0, 0)),
                  full(g1), full(win), full(gq), full(wq), full(gkv), full(wkv),
                  pl.BlockSpec((tm, LANES), lambda bi, i: (i, 0)),
                  pl.BlockSpec((tm, LANES), lambda bi, i: (i, 0))],
        out_specs=(head_out(256), head_out(256), vt_out,
                   head_out(128), head_out(128), vt_out),
        compiler_params=_cparams(("parallel", "parallel")),
        name="pre",
    )(x, mod, g1, win, gq, wq, gkv, wkv, cos_t, sin_t)


def _scores_t(q, k_ref, chunk):
    k = k_ref[0, 0, chunk * ATT_TK:(chunk + 1) * ATT_TK, :]
    return lax.dot_general(k, q, (((1,), (1,)), ((), ())), preferred_element_type=F32)


def _column_max(s):
    n_slabs = s.shape[0] // SUBLANES
    acc = [s[i * SUBLANES:(i + 1) * SUBLANES] for i in range(MAX_CHAINS)]
    for r in range(MAX_CHAINS, n_slabs):
        acc[r % MAX_CHAINS] = jnp.maximum(acc[r % MAX_CHAINS], s[r * SUBLANES:(r + 1) * SUBLANES])
    top = functools.reduce(jnp.maximum, acc)
    return jnp.max(top, axis=0, keepdims=True)


def _softmax_pv_t(s, s_max, vx, m, accx):
    m_new = jnp.maximum(m, s_max)
    alpha = jnp.exp2(m - m_new)
    p = jnp.exp2(s - m_new)
    accx_new = alpha * accx + jnp.dot(vx, p.astype(BF16), preferred_element_type=F32)
    return m_new, accx_new


def _attend_t(q_parts, k_ref, vt_ref, s_ref):
    n_chunks = k_ref.shape[2] // ATT_TK
    n_parts = len(q_parts)
    ahead = ATT_SLOTS - 1
    ones = jnp.ones((ONES_ROWS, ATT_TK), BF16)

    def scores_into(part, chunk):
        s = _scores_t(q_parts[part], k_ref, chunk)
        s_ref[(chunk % ATT_SLOTS) * n_parts + part] = s
        return _column_max(s)

    s_max = {(i, c): scores_into(i, c) for c in range(min(ahead, n_chunks)) for i in range(n_parts)}
    state = [(jnp.full((1, q.shape[0]), NEG_INF, F32), jnp.zeros((LANES + ONES_ROWS, q.shape[0]), F32))
             for q in q_parts]
    for c in range(n_chunks):
        vx = jnp.concatenate([vt_ref[0, 0, :, c * ATT_TK:(c + 1) * ATT_TK], ones], axis=0)
        for i in range(n_parts):
            if c + ahead < n_chunks:
                s_max[(i, c + ahead)] = scores_into(i, c + ahead)
            s = s_ref[(c % ATT_SLOTS) * n_parts + i]
            state[i] = _softmax_pv_t(s, s_max.pop((i, c)), vx, *state[i])
    return [accx for _, accx in state]


def _normalise_t(accx):
    return accx[:LANES] / accx[LANES:LANES + 1]


def _mla_kernel(q_ref, k_ref, vt_ref, o_ref, s_ref):
    (accx,) = _attend_t([q_ref[0, 0]], k_ref, vt_ref, s_ref)
    o_ref[0] = _normalise_t(accx).T.astype(o_ref.dtype)


def _mla_attn(q, k, v):
    b, h, s, _ = q.shape
    tq = ATT_TQ
    return pl.pallas_call(
        _mla_kernel,
        out_shape=jax.ShapeDtypeStruct((b, s, h * MLA_V_DIM), BF16),
        grid=(b, h, s // tq),
        in_specs=[pl.BlockSpec((1, 1, tq, MLA_QK_PAD), lambda bi, hi, i: (bi, hi, i, 0)),
                  pl.BlockSpec((1, 1, s, MLA_QK_PAD), lambda bi, hi, i: (bi, hi, 0, 0)),
                  pl.BlockSpec((1, 1, MLA_V_DIM, s), lambda bi, hi, i: (bi, hi, 0, 0))],
        out_specs=pl.BlockSpec((1, tq, MLA_V_DIM), lambda bi, hi, i: (bi, i, hi)),
        scratch_shapes=[pltpu.VMEM((ATT_SLOTS, ATT_TK, tq), F32)],
        compiler_params=_cparams(("parallel", "parallel", "parallel")),
        name="mla_attn",
    )(q, k, v)


def _diff_kernel(lam_ref, g_ref, q_ref, k_ref, vt_ref, o_ref, s_ref, *, out_scale, lambda_init):
    q = q_ref[0, 0]
    lane = lax.broadcasted_iota(jnp.int32, q.shape, 1)
    zero = jnp.zeros_like(q)
    q1 = jnp.where(lane < DIFF_QK_DIM, q, zero)
    q2 = jnp.where(lane < DIFF_QK_DIM, zero, q)
    (a1,) = _attend_t([q1], k_ref, vt_ref, s_ref.at[0:ATT_SLOTS])
    (a2,) = _attend_t([q2], k_ref, vt_ref, s_ref.at[ATT_SLOTS:2 * ATT_SLOTS])

    lv = lam_ref[...]
    lam = (jnp.exp(jnp.sum(lv[0:1] * lv[1:2], axis=-1, keepdims=True))
           - jnp.exp(jnp.sum(lv[2:3] * lv[3:4], axis=-1, keepdims=True)) + lambda_init)
    o = _normalise_t(a1) - lam * _normalise_t(a2)
    o = o * lax.rsqrt(jnp.mean(o * o, axis=0, keepdims=True) + EPS) * g_ref[...]
    o_ref[0] = (o * out_scale).T.astype(o_ref.dtype)


def _diff_attn(lam_vecs, subln_g, q, k, v, lambda_init):
    b, h, s, _ = q.shape
    tq = ATT_TQ
    kern = functools.partial(_diff_kernel, out_scale=1.0 - lambda_init, lambda_init=lambda_init)
    return pl.pallas_call(
        kern,
        out_shape=jax.ShapeDtypeStruct((b, s, h * DIFF_V_DIM), BF16),
        grid=(b, h, s // tq),
        in_specs=[pl.BlockSpec((4, DIFF_QK_DIM), lambda bi, hi, i: (0, 0)),
                  pl.BlockSpec((DIFF_V_DIM, 1), lambda bi, hi, i: (0, 0)),
                  pl.BlockSpec((1, 1, tq, 128), lambda bi, hi, i: (bi, hi, i, 0)),
                  pl.BlockSpec((1, 1, s, 128), lambda bi, hi, i: (bi, hi, 0, 0)),
                  pl.BlockSpec((1, 1, DIFF_V_DIM, s), lambda bi, hi, i: (bi, hi, 0, 0))],
        out_specs=pl.BlockSpec((1, tq, DIFF_V_DIM), lambda bi, hi, i: (bi, i, hi)),
        scratch_shapes=[pltpu.VMEM((2 * ATT_SLOTS, ATT_TK, tq), F32)],
        compiler_params=_cparams(("parallel", "parallel", "parallel")),
        name="diff_attn",
    )(lam_vecs, subln_g, q, k, v)


def _post_kernel(x_ref, mla_ref, dif_ref, mod_ref, wo_ref, g2_ref, wr_ref, br_ref,
                 x1_ref, h2_ref, ids_ref, ws_ref, cnt_ref, base_ref):
    first = jnp.logical_and(pl.program_id(0) == 0, pl.program_id(1) == 0)

    @pl.when(first)
    def _():
        base_ref[...] = jnp.zeros_like(base_ref)

    tm = x_ref.shape[1]
    gate1 = mod_ref[0, 2:3, :]
    shift2 = mod_ref[0, 3:4, :]
    scale2 = mod_ref[0, 4:5, :]
    mixin = jnp.concatenate([mla_ref[0], dif_ref[0]], axis=-1)
    mix = jnp.dot(mixin, wo_ref[...], preferred_element_type=F32)
    x1 = x_ref[0] + gate1 * mix
    x1_ref[0] = x1
    h2 = _rms(x1, g2_ref[...]) * (1.0 + scale2) + shift2
    _store_token_rows(h2_ref, h2)

    logits = lax.dot_general(wr_ref[...], h2, (((1,), (1,)), ((), ())), preferred_element_type=F32,
                             precision=lax.Precision.HIGHEST) + br_ref[...]
    el = logits[0:N_EXPERTS]
    gl = logits[N_EXPERTS:N_EXPERTS + N_GROUPS]

    grow = lax.broadcasted_iota(jnp.int32, gl.shape, 0).astype(F32)
    gmax = jnp.max(gl, axis=0, keepdims=True)
    gidx = jnp.min(jnp.where(gl == gmax, grow, float(N_GROUPS)), axis=0, keepdims=True)
    gprob = 1.0 / jnp.sum(jnp.exp(gl - gmax), axis=0, keepdims=True)

    erow_i = lax.broadcasted_iota(jnp.int32, el.shape, 0)
    erow = erow_i.astype(F32)
    egrp = (erow_i // EXPERTS_PER_GROUP).astype(F32)
    masked = jnp.where(egrp == gidx, el, NEG_INF)
    l1 = jnp.max(masked, axis=0, keepdims=True)
    i1 = jnp.min(jnp.where(masked == l1, erow, float(N_EXPERTS)), axis=0, keepdims=True)
    oh1 = erow == i1
    masked2 = jnp.where(oh1, NEG_INF, masked)
    l2 = jnp.max(masked2, axis=0, keepdims=True)
    i2 = jnp.min(jnp.where(masked2 == l2, erow, float(N_EXPERTS)), axis=0, keepdims=True)
    oh2 = erow == i2
    e21 = jnp.exp(l2 - l1)
    w1 = gprob * (1.0 / (1.0 + e21))
    w2 = gprob * (e21 / (1.0 + e21))

    oh = jnp.where(jnp.logical_or(oh1, oh2), 1.0, 0.0)
    tr = lax.broadcasted_iota(jnp.int32, (tm, tm), 0)
    tc = lax.broadcasted_iota(jnp.int32, (tm, tm), 1)
    upper = jnp.where(tr < tc, 1.0, 0.0).astype(BF16)
    pos = base_ref[...][:, 0:1] + jnp.dot(oh.astype(BF16), upper, preferred_element_type=F32)
    r1 = jnp.sum(jnp.where(oh1, pos, 0.0), axis=0, keepdims=True)
    r2 = jnp.sum(jnp.where(oh2, pos, 0.0), axis=0, keepdims=True)
    base_new = base_ref[...] + jnp.sum(oh, axis=1, keepdims=True)
    base_ref[...] = base_new
    cnt_ref[...] = base_new

    ids_ref[0] = jnp.concatenate([i1, i2, r1, r2], axis=0).astype(jnp.int32)
    ws_ref[0] = jnp.concatenate([w1, w2], axis=0)


def _post(x, mla_out, d_out, mod, wo, g2, wr, br):
    b, s, d = x.shape
    tm = POST_TM
    nt = s // tm
    full = lambda a: pl.BlockSpec(a.shape, lambda bi, i: (0,) * a.ndim)
    tok = lambda w: pl.BlockSpec((1, tm, w), lambda bi, i: (bi, i, 0))
    return pl.pallas_call(
        _post_kernel,
        out_shape=(jax.ShapeDtypeStruct((b, s, d), F32),
                   jax.ShapeDtypeStruct((b * s * ROW_CHUNKS, LANES), F32),
                   jax.ShapeDtypeStruct((b * nt, 4, tm), jnp.int32),
                   jax.ShapeDtypeStruct((b * nt, 2, tm), F32),
                   jax.ShapeDtypeStruct((N_EXPERTS, LANES), F32)),
        grid=(b, nt),
        in_specs=[tok(d), tok(mla_out.shape[2]), tok(d_out.shape[2]),
                  pl.BlockSpec((1, 6, d), lambda bi, i: (bi, 0, 0)),
                  full(wo), full(g2), full(wr), full(br)],
        out_specs=(tok(d),
                   pl.BlockSpec((tm * ROW_CHUNKS, LANES), lambda bi, i: (bi * nt + i, 0)),
                   pl.BlockSpec((1, 4, tm), lambda bi, i: (bi * nt + i, 0, 0)),
                   pl.BlockSpec((1, 2, tm), lambda bi, i: (bi * nt + i, 0, 0)),
                   pl.BlockSpec((N_EXPERTS, LANES), lambda bi, i: (0, 0))),
        scratch_shapes=[pltpu.VMEM((N_EXPERTS, LANES), F32)],
        compiler_params=_cparams(("arbitrary", "arbitrary")),
        name="post",
    )(x, mla_out, d_out, mod, wo, g2, wr, br)


def _dispatch_kernel(d0_ref, d1_ref, pad_ref, h_ref, xs_ref, zero_ref, sem):
    tm = h_ref.shape[0] // ROW_CHUNKS
    i = pl.program_id(0)

    @pl.when(i == 0)
    def _():
        zero_ref[...] = jnp.zeros_like(zero_ref)
        for e in range(N_EXPERTS):
            pltpu.make_async_copy(zero_ref, _token_tiles(xs_ref, pad_ref[e], EXPERT_TM), sem).start()
        for e in range(N_EXPERTS):
            pltpu.make_async_copy(zero_ref, _token_tiles(xs_ref, pad_ref[e], EXPERT_TM), sem).wait()
        n_rows = xs_ref.shape[0] // ROW_CHUNKS
        for e in range(N_EXPERTS):
            start = pad_ref[N_EXPERTS] + e * EXPERT_TM

            @pl.when(start < n_rows)
            def _():
                pltpu.make_async_copy(zero_ref, _token_tiles(xs_ref, start, EXPERT_TM), sem).start()

        for e in range(N_EXPERTS):
            start = pad_ref[N_EXPERTS] + e * EXPERT_TM

            @pl.when(start < n_rows)
            def _():
                pltpu.make_async_copy(zero_ref, _token_tiles(xs_ref, start, EXPERT_TM), sem).wait()

    base = i * tm

    def issue(r, carry):
        src = _token_tile(h_ref, r)
        pltpu.make_async_copy(src, _token_tile(xs_ref, d0_ref[base + r]), sem).start()
        pltpu.make_async_copy(src, _token_tile(xs_ref, d1_ref[base + r]), sem).start()
        return carry

    lax.fori_loop(0, tm, issue, 0, unroll=8)

    def drain(r, carry):
        src = _token_tile(h_ref, r)
        pltpu.make_async_copy(src, _token_tile(xs_ref, 0), sem).wait()
        pltpu.make_async_copy(src, _token_tile(xs_ref, 0), sem).wait()
        return carry

    lax.fori_loop(0, tm, drain, 0, unroll=8)


def _dispatch(dest0, dest1, pad_start, h2, rows):
    t = h2.shape[0] // ROW_CHUNKS
    tm = DISPATCH_TM
    return pl.pallas_call(
        _dispatch_kernel,
        out_shape=jax.ShapeDtypeStruct((rows * ROW_CHUNKS, LANES), F32),
        grid_spec=pltpu.PrefetchScalarGridSpec(
            num_scalar_prefetch=3,
            grid=(t // tm,),
            in_specs=[pl.BlockSpec((tm * ROW_CHUNKS, LANES), lambda i, *_: (i, 0))],
            out_specs=pl.BlockSpec(memory_space=pl.ANY),
            scratch_shapes=[pltpu.VMEM((EXPERT_TM * ROW_CHUNKS, LANES), F32), pltpu.SemaphoreType.DMA]),
        compiler_params=_cparams(("arbitrary",)),
        name="dispatch",
    )(dest0, dest1, pad_start, h2)


def _expert_kernel(be_ref, nbu_ref, x_ref, wg_ref, wu_ref, wd_ref, y_ref):
    @pl.when(pl.program_id(0) < nbu_ref[0])
    def _():
        x = jnp.concatenate(_load_token_rows(x_ref), axis=-1).astype(BF16)
        g = jnp.dot(x, wg_ref[0], preferred_element_type=F32)
        u = jnp.dot(x, wu_ref[0], preferred_element_type=F32)
        a = (g / (1.0 + jnp.exp(-g))) * u
        y = jnp.dot(a.astype(BF16), wd_ref[0], preferred_element_type=F32)
        _store_token_rows(y_ref, y)

    @pl.when(pl.program_id(0) >= nbu_ref[0])
    def _():
        y_ref[...] = jnp.zeros_like(y_ref)


def _experts(block_expert, nb_used, xs, wg, wu, wd):
    rows = xs.shape[0] // ROW_CHUNKS
    d = D_MODEL
    tm = EXPERT_TM
    nb = rows // tm
    row_map = lambda j, be, nbu: (jnp.minimum(j, nbu[0] - 1), 0)
    w_map = lambda j, be, nbu: (be[j], 0, 0)
    return pl.pallas_call(
        _expert_kernel,
        out_shape=jax.ShapeDtypeStruct((rows * ROW_CHUNKS, LANES), F32),
        grid_spec=pltpu.PrefetchScalarGridSpec(
            num_scalar_prefetch=2,
            grid=(nb,),
            in_specs=[pl.BlockSpec((tm * ROW_CHUNKS, LANES), row_map),
                      pl.BlockSpec((1, d, D_EXPERT), w_map),
                      pl.BlockSpec((1, d, D_EXPERT), w_map),
                      pl.BlockSpec((1, D_EXPERT, d), w_map)],
            out_specs=pl.BlockSpec((tm * ROW_CHUNKS, LANES), lambda j, be, nbu: (j, 0))),
        compiler_params=_cparams(("arbitrary",)),
        name="experts",
    )(block_expert, nb_used, xs, wg, wu, wd)


def _combine_kernel(d0_ref, d1_ref, x1_ref, w_ref, mod_ref, gf_ref, ys_ref, o_ref, y0_ref, y1_ref, sems):
    tm = x1_ref.shape[0]
    i = pl.program_id(0)
    n = pl.num_programs(0)

    def gather(step, slot, start):
        base = step * tm

        def one(r, carry):
            c0 = pltpu.make_async_copy(_token_tile(ys_ref, d0_ref[base + r] if start else 0),
                                       _token_tile(y0_ref.at[slot], r), sems.at[slot])
            c1 = pltpu.make_async_copy(_token_tile(ys_ref, d1_ref[base + r] if start else 0),
                                       _token_tile(y1_ref.at[slot], r), sems.at[slot])
            if start:
                c0.start()
                c1.start()
            else:
                c0.wait()
                c1.wait()
            return carry

        lax.fori_loop(0, tm, one, 0, unroll=8)

    slot = i % 2

    @pl.when(i == 0)
    def _():
        gather(0, 0, True)

    @pl.when(i + 1 < n)
    def _():
        gather(i + 1, 1 - slot, True)

    gather(i, slot, False)

    gate2 = mod_ref[0, 5:6, :]
    w = w_ref[...]
    w0 = w[:, 0:1]
    w1 = w[:, 1:2]
    y0 = _load_token_rows(y0_ref.at[slot])
    y1 = _load_token_rows(y1_ref.at[slot])
    moe = jnp.concatenate([w0 * a + w1 * b for a, b in zip(y0, y1)], axis=-1)
    x2 = x1_ref[...] + gate2 * moe
    o_ref[...] = _rms(x2, gf_ref[...])


def _combine(dest0, dest1, x1, w_tok, mod, gf, ys, s_len):
    t, d = x1.shape
    tm = COMBINE_TM
    per_batch = s_len // tm
    return pl.pallas_call(
        _combine_kernel,
        out_shape=jax.ShapeDtypeStruct((t, d), F32),
        grid_spec=pltpu.PrefetchScalarGridSpec(
            num_scalar_prefetch=2,
            grid=(t // tm,),
            in_specs=[pl.BlockSpec((tm, d), lambda i, *_: (i, 0)),
                      pl.BlockSpec((tm, TOP_K), lambda i, *_: (i, 0)),
                      pl.BlockSpec((1, 6, d), lambda i, *_: (i // per_batch, 0, 0)),
                      pl.BlockSpec((1, d), lambda i, *_: (0, 0)),
                      pl.BlockSpec(memory_space=pl.ANY)],
            out_specs=pl.BlockSpec((tm, d), lambda i, *_: (i, 0)),
            scratch_shapes=[pltpu.VMEM((2, tm * ROW_CHUNKS, LANES), F32),
                            pltpu.VMEM((2, tm * ROW_CHUNKS, LANES), F32),
                            pltpu.SemaphoreType.DMA((2,))]),
        compiler_params=_cparams(("arbitrary",)),
        name="combine",
    )(dest0, dest1, x1, w_tok, mod, gf, ys)


def _rope_tables(s_len):
    half = MLA_ROPE_DIM // 2
    inv = ROPE_THETA ** (-jnp.arange(0, MLA_ROPE_DIM, 2, dtype=F32) / MLA_ROPE_DIM)
    ang = jnp.arange(s_len, dtype=jnp.int32).astype(F32)[:, None] * inv[None, :]
    cos, sin = jnp.cos(ang), jnp.sin(ang)
    reps = LANES // MLA_ROPE_DIM
    cos_t = jnp.tile(jnp.concatenate([cos, cos], axis=1), (1, reps))
    sin_t = jnp.tile(jnp.concatenate([-sin, sin], axis=1), (1, reps))
    del half
    return cos_t, sin_t


def _layer(l, x, mod, norm1_g, w_in, q_a_norm_g, w_q_up, kv_a_norm_g, w_kv_up, lambda_q1, lambda_k1,
           lambda_q2, lambda_k2, subln_g, w_out, norm2_g, w_router_group, b_router_group, w_router_expert,
           b_router_expert, w_expert_gate, w_expert_up, w_expert_down, final_norm_g, cos_t, sin_t):
    b, s, d = x.shape
    t = b * s

    wi = w_in[l]
    win = jnp.concatenate([wi[:, 0:512], wi[:, 576:2112], wi[:, 512:576], jnp.zeros((d, 64), F32)],
                          axis=1).astype(BF16)
    wq4 = w_q_up[l].reshape(MLA_Q_RANK, MLA_HEADS, MLA_NOPE_DIM + MLA_ROPE_DIM)
    wq_pe = jnp.pad(wq4[:, :, MLA_NOPE_DIM:], ((0, 0), (0, 0), (0, 64)))
    wq = jnp.concatenate([wq4[:, :, :MLA_NOPE_DIM].reshape(MLA_Q_RANK, 512),
                          wq_pe.reshape(MLA_Q_RANK, 512)], axis=1).astype(BF16)
    wkv4 = w_kv_up[l].reshape(MLA_KV_RANK, MLA_HEADS, MLA_NOPE_DIM + MLA_V_DIM)
    wkv = jnp.concatenate([wkv4[:, :, :MLA_NOPE_DIM].reshape(MLA_KV_RANK, 512),
                           wkv4[:, :, MLA_NOPE_DIM:].reshape(MLA_KV_RANK, 512)], axis=1).astype(BF16)

    qm, km, vm, dq, dk, dv = _pre(x, mod, norm1_g[l][None], win, q_a_norm_g[l][None], wq,
                                  kv_a_norm_g[l][None], wkv, cos_t, sin_t)

    mla_out = _mla_attn(qm, km, vm)
    lambda_init = 0.8 - 0.6 * math.exp(-0.3 * l)
    lam_vecs = jnp.stack([lambda_q1[l], lambda_k1[l], lambda_q2[l], lambda_k2[l]]).astype(F32)
    d_out = _diff_attn(lam_vecs, subln_g[l][:, None], dq, dk, dv, lambda_init)

    wr = jnp.concatenate([w_router_expert[l].T, w_router_group[l].T,
                          jnp.zeros((ROUTER_ROWS - N_EXPERTS - N_GROUPS, d), F32)], axis=0)
    br = jnp.concatenate([b_router_expert[l], b_router_group[l],
                          jnp.zeros((ROUTER_ROWS - N_EXPERTS - N_GROUPS,), F32)])[:, None]
    x1, h2, ids, ws, cnt = _post(x, mla_out, d_out, mod, w_out[l].astype(BF16), norm2_g[l][None], wr, br)

    tmx = EXPERT_TM
    counts = cnt[:, 0].astype(jnp.int32)
    padded = ((counts + tmx - 1) // tmx) * tmx
    pends = jnp.cumsum(padded)
    pstarts = pends - padded
    rows = t * TOP_K + N_EXPERTS * tmx
    nb = rows // tmx
    block_row = jnp.arange(nb, dtype=jnp.int32) * tmx
    block_expert = jnp.minimum(jnp.sum((pends[None, :] <= block_row[:, None]).astype(jnp.int32), axis=1),
                               N_EXPERTS - 1).astype(jnp.int32)
    nb_used = (pends[-1:] // tmx).astype(jnp.int32)
    pad_start = jnp.concatenate([pstarts + counts, pends[-1:]]).astype(jnp.int32)

    ids_t = ids.transpose(1, 0, 2).reshape(4, t)
    dest0 = (pstarts[ids_t[0]] + ids_t[2]).astype(jnp.int32)
    dest1 = (pstarts[ids_t[1]] + ids_t[3]).astype(jnp.int32)
    w_tok = ws.transpose(0, 2, 1).reshape(t, TOP_K)

    xs = _dispatch(dest0, dest1, pad_start, h2, rows)
    ys = _experts(block_expert, nb_used, xs, w_expert_gate[l].astype(BF16), w_expert_up[l].astype(BF16),
                  w_expert_down[l].astype(BF16))
    out = _combine(dest0, dest1, x1.reshape(t, d), w_tok, mod, final_norm_g, ys, s)
    return out.reshape(b, s, d)


def kernel(x, c, w_ada, b_ada, norm1_g, w_in, q_a_norm_g, w_q_up, kv_a_norm_g, w_kv_up, lambda_q1, lambda_k1,
           lambda_q2, lambda_k2, subln_g, w_out, norm2_g, w_router_group, b_router_group, w_router_expert,
           b_router_expert, w_expert_gate, w_expert_up, w_expert_down, final_norm_g):
    b, s, d = x.shape
    depth = w_ada.shape[0]
    assert depth == 1, "the final RMSNorm is fused into the last layer's combine step"
    cos_t, sin_t = _rope_tables(s)
    mod = _ada(c, w_ada[0], b_ada[0]).reshape(b, 6, d)
    return _layer(0, x, mod, norm1_g, w_in, q_a_norm_g, w_q_up, kv_a_norm_g, w_kv_up, lambda_q1, lambda_k1,
                  lambda_q2, lambda_k2, subln_g, w_out, norm2_g, w_router_group, b_router_group,
                  w_router_expert, b_router_expert, w_expert_gate, w_expert_up, w_expert_down,
                  final_norm_g[None], cos_t, sin_t)
```

```python
import functools
import math

import jax
import jax.numpy as jnp
from jax import lax
from jax.experimental import pallas as pl
from jax.experimental.pallas import tpu as pltpu

D_MODEL = 1024
ROPE_THETA = 10000.0
EPS = 1e-6

MLA_HEADS = 4
MLA_Q_RANK = 256
MLA_KV_RANK = 256
MLA_NOPE_DIM = 128
MLA_ROPE_DIM = 64
MLA_V_DIM = 128
MLA_QK_PAD = 256

DIFF_HEADS = 4
DIFF_QK_DIM = 64
DIFF_V_DIM = 128

N_GROUPS = 4
EXPERTS_PER_GROUP = 8
N_EXPERTS = 32
TOP_K = 2
D_EXPERT = 512

LANES = 128
ROW_CHUNKS = D_MODEL // LANES
ROUTER_ROWS = 40

VMEM_LIMIT = 56 * 1024 * 1024

PRE_TM = 512
ATT_TQ = 512
ATT_TK = 1024
ATT_SLOTS = 3
LOG2E = math.log2(math.e)
POST_TM = 512
DISPATCH_TM = 512
EXPERT_TM = 512
COMBINE_TM = 256

F32 = jnp.float32
BF16 = jnp.bfloat16
NEG_INF = float("-inf")


def _cparams(sem):
    return pltpu.CompilerParams(dimension_semantics=sem, vmem_limit_bytes=VMEM_LIMIT)


def _ada_kernel(c_ref, w_ref, b_ref, o_ref):
    c = c_ref[...]
    sc = c / (1.0 + jnp.exp(-c))
    o_ref[...] = jnp.dot(sc, w_ref[...], preferred_element_type=F32,
                         precision=lax.Precision.HIGHEST) + b_ref[...]


def _ada(c, w_ada, b_ada):
    b, d = c.shape
    n = w_ada.shape[1]
    tn = 1536
    return pl.pallas_call(
        _ada_kernel,
        out_shape=jax.ShapeDtypeStruct((b, n), F32),
        grid=(n // tn,),
        in_specs=[pl.BlockSpec((b, d), lambda j: (0, 0)),
                  pl.BlockSpec((d, tn), lambda j: (0, j)),
                  pl.BlockSpec((1, tn), lambda j: (0, j))],
        out_specs=pl.BlockSpec((b, tn), lambda j: (0, j)),
        compiler_params=_cparams(("parallel",)),
        name="ada",
    )(c, w_ada, b_ada.reshape(1, n))


def _rms(x, g):
    return x * lax.rsqrt(jnp.mean(x * x, axis=-1, keepdims=True) + EPS) * g


def _token_tile(ref, t):
    return ref.at[pl.ds(pl.multiple_of(t * ROW_CHUNKS, ROW_CHUNKS), ROW_CHUNKS), :]


def _token_tiles(ref, t, n):
    return ref.at[pl.ds(pl.multiple_of(t * ROW_CHUNKS, ROW_CHUNKS), n * ROW_CHUNKS), :]


def _load_token_rows(ref):
    tm = ref.shape[0] // ROW_CHUNKS
    return [ref[pl.ds(c, tm, stride=ROW_CHUNKS), :] for c in range(ROW_CHUNKS)]


def _store_token_rows(ref, val):
    tm = ref.shape[0] // ROW_CHUNKS
    for c in range(ROW_CHUNKS):
        ref[pl.ds(c, tm, stride=ROW_CHUNKS), :] = val[:, c * LANES:(c + 1) * LANES]


def _rope_tile(x, cos, sin_signed, first_half):
    fwd = pltpu.roll(x, 96, axis=1)
    bwd = pltpu.roll(x, 32, axis=1)
    return x * cos + jnp.where(first_half, fwd, bwd) * sin_signed


def _pre_kernel(x_ref, mod_ref, g1_ref, win_ref, gq_ref, wq_ref, gkv_ref, wkv_ref, cos_ref, sin_ref,
                qm_ref, km_ref, vm_ref, dq_ref, dk_ref, dv_ref):
    x = x_ref[0]
    shift1 = mod_ref[0, 0:1, :]
    scale1 = mod_ref[0, 1:2, :]
    h = _rms(x, g1_ref[...]) * (1.0 + scale1) + shift1
    z = jnp.dot(h.astype(BF16), win_ref[...], preferred_element_type=F32)

    cos = cos_ref[...]
    sin = sin_ref[...]
    lane = lax.broadcasted_iota(jnp.int32, cos.shape, 1)
    first_half = (lane % 64) < 32

    cq = _rms(z[:, 0:256], gq_ref[...])
    q = jnp.dot(cq.astype(BF16), wq_ref[...], preferred_element_type=F32)
    ckv = _rms(z[:, 256:512], gkv_ref[...])
    kv = jnp.dot(ckv.astype(BF16), wkv_ref[...], preferred_element_type=F32)
    kpe = _rope_tile(z[:, 2048:2176], cos, sin, first_half).astype(BF16)

    q_scale = LOG2E * (MLA_NOPE_DIM + MLA_ROPE_DIM) ** -0.5
    for hd in range(MLA_HEADS):
        qn = q[:, hd * 128:(hd + 1) * 128] * q_scale
        qp = _rope_tile(q[:, 512 + hd * 128:512 + (hd + 1) * 128], cos, sin, first_half) * q_scale
        qm_ref[0, hd, :, 0:128] = qn.astype(BF16)
        qm_ref[0, hd, :, 128:256] = qp.astype(BF16)
        km_ref[0, hd, :, 0:128] = kv[:, hd * 128:(hd + 1) * 128].astype(BF16)
        km_ref[0, hd, :, 128:256] = kpe
        vm_ref[0, hd] = kv[:, 512 + hd * 128:512 + (hd + 1) * 128].astype(BF16)

    d_scale = LOG2E * DIFF_QK_DIM ** -0.5
    for hd in range(DIFF_HEADS):
        dq = _rope_tile(z[:, 512 + hd * 128:512 + (hd + 1) * 128], cos, sin, first_half) * d_scale
        dk = _rope_tile(z[:, 1024 + hd * 128:1024 + (hd + 1) * 128], cos, sin, first_half)
        dq_ref[0, hd] = dq.astype(BF16)
        dk_ref[0, hd] = dk.astype(BF16)
        dv_ref[0, hd] = z[:, 1536 + hd * 128:1536 + (hd + 1) * 128].astype(BF16)


def _pre(x, mod, g1, win, gq, wq, gkv, wkv, cos_t, sin_t):
    b, s, d = x.shape
    tm = PRE_TM
    full = lambda a: pl.BlockSpec(a.shape, lambda bi, i: (0,) * a.ndim)
    head_out = lambda w: pl.BlockSpec((1, 4, tm, w), lambda bi, i: (bi, 0, i, 0))
    head_shape = lambda w: jax.ShapeDtypeStruct((b, 4, s, w), BF16)
    return pl.pallas_call(
        _pre_kernel,
        out_shape=(head_shape(256), head_shape(256), head_shape(128),
                   head_shape(128), head_shape(128), head_shape(128)),
        grid=(b, s // tm),
        in_specs=[pl.BlockSpec((1, tm, d), lambda bi, i: (bi, i, 0)),
                  pl.BlockSpec((1, 6, d), lambda bi, i: (bi, 0, 0)),
                  full(g1), full(win), full(gq), full(wq), full(gkv), full(wkv),
                  pl.BlockSpec((tm, LANES), lambda bi, i: (i, 0)),
                  pl.BlockSpec((tm, LANES), lambda bi, i: (i, 0))],
        out_specs=(head_out(256), head_out(256), head_out(128),
                   head_out(128), head_out(128), head_out(128)),
        compiler_params=_cparams(("parallel", "parallel")),
        name="pre",
    )(x, mod, g1, win, gq, wq, gkv, wkv, cos_t, sin_t)


def _scores(q, k_ref, chunk):
    k = k_ref[0, 0, chunk * ATT_TK:(chunk + 1) * ATT_TK, :]
    return lax.dot_general(q, k, (((1,), (1,)), ((), ())), preferred_element_type=F32)


def _softmax_pv(s, vx, m, accx):
    m_new = jnp.maximum(m, jnp.max(s, axis=-1, keepdims=True))
    alpha = jnp.exp2(m - m_new)
    p = jnp.exp2(s - m_new)
    accx_new = alpha * accx + jnp.dot(p.astype(BF16), vx, preferred_element_type=F32)
    return m_new, accx_new


def _attend(q_parts, k_ref, v_ref, s_ref):
    n_chunks = k_ref.shape[2] // ATT_TK
    n_parts = len(q_parts)
    ahead = ATT_SLOTS - 1
    ones = jnp.ones((ATT_TK, LANES), BF16)

    def scores_into(part, chunk):
        s_ref[(chunk % ATT_SLOTS) * n_parts + part] = _scores(q_parts[part], k_ref, chunk)

    for c in range(min(ahead, n_chunks)):
        for i in range(n_parts):
            scores_into(i, c)
    state = [(jnp.full((q.shape[0], 1), NEG_INF, F32), jnp.zeros((q.shape[0], 2 * LANES), F32))
             for q in q_parts]
    for c in range(n_chunks):
        vx = jnp.concatenate([v_ref[0, 0, c * ATT_TK:(c + 1) * ATT_TK, :], ones], axis=-1)
        for i in range(n_parts):
            if c + ahead < n_chunks:
                scores_into(i, c + ahead)
            state[i] = _softmax_pv(s_ref[(c % ATT_SLOTS) * n_parts + i], vx, *state[i])
    return [accx for _, accx in state]


def _normalise(accx):
    return accx[:, :LANES] / accx[:, LANES:]


def _mla_kernel(q_ref, k_ref, v_ref, o_ref, s_ref):
    (accx,) = _attend([q_ref[0, 0]], k_ref, v_ref, s_ref)
    o_ref[0] = _normalise(accx).astype(o_ref.dtype)


def _mla_attn(q, k, v):
    b, h, s, _ = q.shape
    tq = ATT_TQ
    return pl.pallas_call(
        _mla_kernel,
        out_shape=jax.ShapeDtypeStruct((b, s, h * MLA_V_DIM), BF16),
        grid=(b, h, s // tq),
        in_specs=[pl.BlockSpec((1, 1, tq, MLA_QK_PAD), lambda bi, hi, i: (bi, hi, i, 0)),
                  pl.BlockSpec((1, 1, s, MLA_QK_PAD), lambda bi, hi, i: (bi, hi, 0, 0)),
                  pl.BlockSpec((1, 1, s, MLA_V_DIM), lambda bi, hi, i: (bi, hi, 0, 0))],
        out_specs=pl.BlockSpec((1, tq, MLA_V_DIM), lambda bi, hi, i: (bi, i, hi)),
        scratch_shapes=[pltpu.VMEM((ATT_SLOTS, tq, ATT_TK), F32)],
        compiler_params=_cparams(("parallel", "parallel", "parallel")),
        name="mla_attn",
    )(q, k, v)


def _diff_kernel(lam_ref, g_ref, q_ref, k_ref, v_ref, o_ref, s_ref, *, out_scale, lambda_init):
    q = q_ref[0, 0]
    lane = lax.broadcasted_iota(jnp.int32, q.shape, 1)
    zero = jnp.zeros_like(q)
    q1 = jnp.where(lane < DIFF_QK_DIM, q, zero)
    q2 = jnp.where(lane < DIFF_QK_DIM, zero, q)
    (a1,) = _attend([q1], k_ref, v_ref, s_ref.at[0:ATT_SLOTS])
    (a2,) = _attend([q2], k_ref, v_ref, s_ref.at[ATT_SLOTS:2 * ATT_SLOTS])

    lv = lam_ref[...]
    lam = (jnp.exp(jnp.sum(lv[0:1] * lv[1:2], axis=-1, keepdims=True))
           - jnp.exp(jnp.sum(lv[2:3] * lv[3:4], axis=-1, keepdims=True)) + lambda_init)
    o = _normalise(a1) - lam * _normalise(a2)
    o_ref[0] = (_rms(o, g_ref[...]) * out_scale).astype(o_ref.dtype)


def _diff_attn(lam_vecs, subln_g, q, k, v, lambda_init):
    b, h, s, _ = q.shape
    tq = ATT_TQ
    kern = functools.partial(_diff_kernel, out_scale=1.0 - lambda_init, lambda_init=lambda_init)
    return pl.pallas_call(
        kern,
        out_shape=jax.ShapeDtypeStruct((b, s, h * DIFF_V_DIM), BF16),
        grid=(b, h, s // tq),
        in_specs=[pl.BlockSpec((4, DIFF_QK_DIM), lambda bi, hi, i: (0, 0)),
                  pl.BlockSpec((1, DIFF_V_DIM), lambda bi, hi, i: (0, 0)),
                  pl.BlockSpec((1, 1, tq, 128), lambda bi, hi, i: (bi, hi, i, 0)),
                  pl.BlockSpec((1, 1, s, 128), lambda bi, hi, i: (bi, hi, 0, 0)),
                  pl.BlockSpec((1, 1, s, DIFF_V_DIM), lambda bi, hi, i: (bi, hi, 0, 0))],
        out_specs=pl.BlockSpec((1, tq, DIFF_V_DIM), lambda bi, hi, i: (bi, i, hi)),
        scratch_shapes=[pltpu.VMEM((2 * ATT_SLOTS, tq, ATT_TK), F32)],
        compiler_params=_cparams(("parallel", "parallel", "parallel")),
        name="diff_attn",
    )(lam_vecs, subln_g, q, k, v)


def _post_kernel(x_ref, mla_ref, dif_ref, mod_ref, wo_ref, g2_ref, wr_ref, br_ref,
                 x1_ref, h2_ref, ids_ref, ws_ref, cnt_ref, base_ref):
    first = jnp.logical_and(pl.program_id(0) == 0, pl.program_id(1) == 0)

    @pl.when(first)
    def _():
        base_ref[...] = jnp.zeros_like(base_ref)

    tm = x_ref.shape[1]
    gate1 = mod_ref[0, 2:3, :]
    shift2 = mod_ref[0, 3:4, :]
    scale2 = mod_ref[0, 4:5, :]
    mixin = jnp.concatenate([mla_ref[0], dif_ref[0]], axis=-1)
    mix = jnp.dot(mixin, wo_ref[...], preferred_element_type=F32)
    x1 = x_ref[0] + gate1 * mix
    x1_ref[0] = x1
    h2 = _rms(x1, g2_ref[...]) * (1.0 + scale2) + shift2
    _store_token_rows(h2_ref, h2)

    logits = lax.dot_general(wr_ref[...], h2, (((1,), (1,)), ((), ())), preferred_element_type=F32,
                             precision=lax.Precision.HIGHEST) + br_ref[...]
    el = logits[0:N_EXPERTS]
    gl = logits[N_EXPERTS:N_EXPERTS + N_GROUPS]

    grow = lax.broadcasted_iota(jnp.int32, gl.shape, 0).astype(F32)
    gmax = jnp.max(gl, axis=0, keepdims=True)
    gidx = jnp.min(jnp.where(gl == gmax, grow, float(N_GROUPS)), axis=0, keepdims=True)
    gprob = 1.0 / jnp.sum(jnp.exp(gl - gmax), axis=0, keepdims=True)

    erow_i = lax.broadcasted_iota(jnp.int32, el.shape, 0)
    erow = erow_i.astype(F32)
    egrp = (erow_i // EXPERTS_PER_GROUP).astype(F32)
    masked = jnp.where(egrp == gidx, el, NEG_INF)
    l1 = jnp.max(masked, axis=0, keepdims=True)
    i1 = jnp.min(jnp.where(masked == l1, erow, float(N_EXPERTS)), axis=0, keepdims=True)
    oh1 = erow == i1
    masked2 = jnp.where(oh1, NEG_INF, masked)
    l2 = jnp.max(masked2, axis=0, keepdims=True)
    i2 = jnp.min(jnp.where(masked2 == l2, erow, float(N_EXPERTS)), axis=0, keepdims=True)
    oh2 = erow == i2
    e21 = jnp.exp(l2 - l1)
    w1 = gprob * (1.0 / (1.0 + e21))
    w2 = gprob * (e21 / (1.0 + e21))

    oh = jnp.where(jnp.logical_or(oh1, oh2), 1.0, 0.0)
    tr = lax.broadcasted_iota(jnp.int32, (tm, tm), 0)
    tc = lax.broadcasted_iota(jnp.int32, (tm, tm), 1)
    upper = jnp.where(tr < tc, 1.0, 0.0).astype(BF16)
    pos = base_ref[...][:, 0:1] + jnp.dot(oh.astype(BF16), upper, preferred_element_type=F32)
    r1 = jnp.sum(jnp.where(oh1, pos, 0.0), axis=0, keepdims=True)
    r2 = jnp.sum(jnp.where(oh2, pos, 0.0), axis=0, keepdims=True)
    base_new = base_ref[...] + jnp.sum(oh, axis=1, keepdims=True)
    base_ref[...] = base_new
    cnt_ref[...] = base_new

    ids_ref[0] = jnp.concatenate([i1, i2, r1, r2], axis=0).astype(jnp.int32)
    ws_ref[0] = jnp.concatenate([w1, w2], axis=0)


def _post(x, mla_out, d_out, mod, wo, g2, wr, br):
    b, s, d = x.shape
    tm = POST_TM
    nt = s // tm
    full = lambda a: pl.BlockSpec(a.shape, lambda bi, i: (0,) * a.ndim)
    tok = lambda w: pl.BlockSpec((1, tm, w), lambda bi, i: (bi, i, 0))
    return pl.pallas_call(
        _post_kernel,
        out_shape=(jax.ShapeDtypeStruct((b, s, d), F32),
                   jax.ShapeDtypeStruct((b * s * ROW_CHUNKS, LANES), F32),
                   jax.ShapeDtypeStruct((b * nt, 4, tm), jnp.int32),
                   jax.ShapeDtypeStruct((b * nt, 2, tm), F32),
                   jax.ShapeDtypeStruct((N_EXPERTS, LANES), F32)),
        grid=(b, nt),
        in_specs=[tok(d), tok(mla_out.shape[2]), tok(d_out.shape[2]),
                  pl.BlockSpec((1, 6, d), lambda bi, i: (bi, 0, 0)),
                  full(wo), full(g2), full(wr), full(br)],
        out_specs=(tok(d),
                   pl.BlockSpec((tm * ROW_CHUNKS, LANES), lambda bi, i: (bi * nt + i, 0)),
                   pl.BlockSpec((1, 4, tm), lambda bi, i: (bi * nt + i, 0, 0)),
                   pl.BlockSpec((1, 2, tm), lambda bi, i: (bi * nt + i, 0, 0)),
                   pl.BlockSpec((N_EXPERTS, LANES), lambda bi, i: (0, 0))),
        scratch_shapes=[pltpu.VMEM((N_EXPERTS, LANES), F32)],
        compiler_params=_cparams(("arbitrary", "arbitrary")),
        name="post",
    )(x, mla_out, d_out, mod, wo, g2, wr, br)


def _dispatch_kernel(d0_ref, d1_ref, pad_ref, h_ref, xs_ref, zero_ref, sem):
    tm = h_ref.shape[0] // ROW_CHUNKS
    i = pl.program_id(0)

    @pl.when(i == 0)
    def _():
        zero_ref[...] = jnp.zeros_like(zero_ref)
        for e in range(N_EXPERTS):
            pltpu.make_async_copy(zero_ref, _token_tiles(xs_ref, pad_ref[e], EXPERT_TM), sem).start()
        for e in range(N_EXPERTS):
            pltpu.make_async_copy(zero_ref, _token_tiles(xs_ref, pad_ref[e], EXPERT_TM), sem).wait()
        n_rows = xs_ref.shape[0] // ROW_CHUNKS
        for e in range(N_EXPERTS):
            start = pad_ref[N_EXPERTS] + e * EXPERT_TM

            @pl.when(start < n_rows)
            def _():
                pltpu.make_async_copy(zero_ref, _token_tiles(xs_ref, start, EXPERT_TM), sem).start()

        for e in range(N_EXPERTS):
            start = pad_ref[N_EXPERTS] + e * EXPERT_TM

            @pl.when(start < n_rows)
            def _():
                pltpu.make_async_copy(zero_ref, _token_tiles(xs_ref, start, EXPERT_TM), sem).wait()

    base = i * tm

    def issue(r, carry):
        src = _token_tile(h_ref, r)
        pltpu.make_async_copy(src, _token_tile(xs_ref, d0_ref[base + r]), sem).start()
        pltpu.make_async_copy(src, _token_tile(xs_ref, d1_ref[base + r]), sem).start()
        return carry

    lax.fori_loop(0, tm, issue, 0, unroll=8)

    def drain(r, carry):
        src = _token_tile(h_ref, r)
        pltpu.make_async_copy(src, _token_tile(xs_ref, 0), sem).wait()
        pltpu.make_async_copy(src, _token_tile(xs_ref, 0), sem).wait()
        return carry

    lax.fori_loop(0, tm, drain, 0, unroll=8)


def _dispatch(dest0, dest1, pad_start, h2, rows):
    t = h2.shape[0] // ROW_CHUNKS
    tm = DISPATCH_TM
    return pl.pallas_call(
        _dispatch_kernel,
        out_shape=jax.ShapeDtypeStruct((rows * ROW_CHUNKS, LANES), F32),
        grid_spec=pltpu.PrefetchScalarGridSpec(
            num_scalar_prefetch=3,
            grid=(t // tm,),
            in_specs=[pl.BlockSpec((tm * ROW_CHUNKS, LANES), lambda i, *_: (i, 0))],
            out_specs=pl.BlockSpec(memory_space=pl.ANY),
            scratch_shapes=[pltpu.VMEM((EXPERT_TM * ROW_CHUNKS, LANES), F32), pltpu.SemaphoreType.DMA]),
        compiler_params=_cparams(("arbitrary",)),
        name="dispatch",
    )(dest0, dest1, pad_start, h2)


def _expert_kernel(be_ref, nbu_ref, x_ref, wg_ref, wu_ref, wd_ref, y_ref):
    @pl.when(pl.program_id(0) < nbu_ref[0])
    def _():
        x = jnp.concatenate(_load_token_rows(x_ref), axis=-1).astype(BF16)
        g = jnp.dot(x, wg_ref[0], preferred_element_type=F32)
        u = jnp.dot(x, wu_ref[0], preferred_element_type=F32)
        a = (g / (1.0 + jnp.exp(-g))) * u
        y = jnp.dot(a.astype(BF16), wd_ref[0], preferred_element_type=F32)
        _store_token_rows(y_ref, y)

    @pl.when(pl.program_id(0) >= nbu_ref[0])
    def _():
        y_ref[...] = jnp.zeros_like(y_ref)


def _experts(block_expert, nb_used, xs, wg, wu, wd):
    rows = xs.shape[0] // ROW_CHUNKS
    d = D_MODEL
    tm = EXPERT_TM
    nb = rows // tm
    row_map = lambda j, be, nbu: (jnp.minimum(j, nbu[0] - 1), 0)
    w_map = lambda j, be, nbu: (be[j], 0, 0)
    return pl.pallas_call(
        _expert_kernel,
        out_shape=jax.ShapeDtypeStruct((rows * ROW_CHUNKS, LANES), F32),
        grid_spec=pltpu.PrefetchScalarGridSpec(
            num_scalar_prefetch=2,
            grid=(nb,),
            in_specs=[pl.BlockSpec((tm * ROW_CHUNKS, LANES), row_map),
                      pl.BlockSpec((1, d, D_EXPERT), w_map),
                      pl.BlockSpec((1, d, D_EXPERT), w_map),
                      pl.BlockSpec((1, D_EXPERT, d), w_map)],
            out_specs=pl.BlockSpec((tm * ROW_CHUNKS, LANES), lambda j, be, nbu: (j, 0))),
        compiler_params=_cparams(("arbitrary",)),
        name="experts",
    )(block_expert, nb_used, xs, wg, wu, wd)


def _combine_kernel(d0_ref, d1_ref, x1_ref, w_ref, mod_ref, gf_ref, ys_ref, o_ref, y0_ref, y1_ref, sems):
    tm = x1_ref.shape[0]
    i = pl.program_id(0)
    n = pl.num_programs(0)

    def gather(step, slot, start):
        base = step * tm

        def one(r, carry):
            c0 = pltpu.make_async_copy(_token_tile(ys_ref, d0_ref[base + r] if start else 0),
                                       _token_tile(y0_ref.at[slot], r), sems.at[slot])
            c1 = pltpu.make_async_copy(_token_tile(ys_ref, d1_ref[base + r] if start else 0),
                                       _token_tile(y1_ref.at[slot], r), sems.at[slot])
            if start:
                c0.start()
                c1.start()
            else:
                c0.wait()
                c1.wait()
            return carry

        lax.fori_loop(0, tm, one, 0, unroll=8)

    slot = i % 2

    @pl.when(i == 0)
    def _():
        gather(0, 0, True)

    @pl.when(i + 1 < n)
    def _():
        gather(i + 1, 1 - slot, True)

    gather(i, slot, False)

    gate2 = mod_ref[0, 5:6, :]
    w = w_ref[...]
    w0 = w[:, 0:1]
    w1 = w[:, 1:2]
    y0 = _load_token_rows(y0_ref.at[slot])
    y1 = _load_token_rows(y1_ref.at[slot])
    moe = jnp.concatenate([w0 * a + w1 * b for a, b in zip(y0, y1)], axis=-1)
    x2 = x1_ref[...] + gate2 * moe
    o_ref[...] = _rms(x2, gf_ref[...])


def _combine(dest0, dest1, x1, w_tok, mod, gf, ys, s_len):
    t, d = x1.shape
    tm = COMBINE_TM
    per_batch = s_len // tm
    return pl.pallas_call(
        _combine_kernel,
        out_shape=jax.ShapeDtypeStruct((t, d), F32),
        grid_spec=pltpu.PrefetchScalarGridSpec(
            num_scalar_prefetch=2,
            grid=(t // tm,),
            in_specs=[pl.BlockSpec((tm, d), lambda i, *_: (i, 0)),
                      pl.BlockSpec((tm, TOP_K), lambda i, *_: (i, 0)),
                      pl.BlockSpec((1, 6, d), lambda i, *_: (i // per_batch, 0, 0)),
                      pl.BlockSpec((1, d), lambda i, *_: (0, 0)),
                      pl.BlockSpec(memory_space=pl.ANY)],
            out_specs=pl.BlockSpec((tm, d), lambda i, *_: (i, 0)),
            scratch_shapes=[pltpu.VMEM((2, tm * ROW_CHUNKS, LANES), F32),
                            pltpu.VMEM((2, tm * ROW_CHUNKS, LANES), F32),
                            pltpu.SemaphoreType.DMA((2,))]),
        compiler_params=_cparams(("arbitrary",)),
        name="combine",
    )(dest0, dest1, x1, w_tok, mod, gf, ys)


def _rope_tables(s_len):
    half = MLA_ROPE_DIM // 2
    inv = ROPE_THETA ** (-jnp.arange(0, MLA_ROPE_DIM, 2, dtype=F32) / MLA_ROPE_DIM)
    ang = jnp.arange(s_len, dtype=jnp.int32).astype(F32)[:, None] * inv[None, :]
    cos, sin = jnp.cos(ang), jnp.sin(ang)
    reps = LANES // MLA_ROPE_DIM
    cos_t = jnp.tile(jnp.concatenate([cos, cos], axis=1), (1, reps))
    sin_t = jnp.tile(jnp.concatenate([-sin, sin], axis=1), (1, reps))
    del half
    return cos_t, sin_t


def _layer(l, x, mod, norm1_g, w_in, q_a_norm_g, w_q_up, kv_a_norm_g, w_kv_up, lambda_q1, lambda_k1,
           lambda_q2, lambda_k2, subln_g, w_out, norm2_g, w_router_group, b_router_group, w_router_expert,
           b_router_expert, w_expert_gate, w_expert_up, w_expert_down, final_norm_g, cos_t, sin_t):
    b, s, d = x.shape
    t = b * s

    wi = w_in[l]
    win = jnp.concatenate([wi[:, 0:512], wi[:, 576:2112], wi[:, 512:576], jnp.zeros((d, 64), F32)],
                          axis=1).astype(BF16)
    wq4 = w_q_up[l].reshape(MLA_Q_RANK, MLA_HEADS, MLA_NOPE_DIM + MLA_ROPE_DIM)
    wq_pe = jnp.pad(wq4[:, :, MLA_NOPE_DIM:], ((0, 0), (0, 0), (0, 64)))
    wq = jnp.concatenate([wq4[:, :, :MLA_NOPE_DIM].reshape(MLA_Q_RANK, 512),
                          wq_pe.reshape(MLA_Q_RANK, 512)], axis=1).astype(BF16)
    wkv4 = w_kv_up[l].reshape(MLA_KV_RANK, MLA_HEADS, MLA_NOPE_DIM + MLA_V_DIM)
    wkv = jnp.concatenate([wkv4[:, :, :MLA_NOPE_DIM].reshape(MLA_KV_RANK, 512),
                           wkv4[:, :, MLA_NOPE_DIM:].reshape(MLA_KV_RANK, 512)], axis=1).astype(BF16)

    qm, km, vm, dq, dk, dv = _pre(x, mod, norm1_g[l][None], win, q_a_norm_g[l][None], wq,
                                  kv_a_norm_g[l][None], wkv, cos_t, sin_t)

    mla_out = _mla_attn(qm, km, vm)
    lambda_init = 0.8 - 0.6 * math.exp(-0.3 * l)
    lam_vecs = jnp.stack([lambda_q1[l], lambda_k1[l], lambda_q2[l], lambda_k2[l]]).astype(F32)
    d_out = _diff_attn(lam_vecs, subln_g[l][None], dq, dk, dv, lambda_init)

    wr = jnp.concatenate([w_router_expert[l].T, w_router_group[l].T,
                          jnp.zeros((ROUTER_ROWS - N_EXPERTS - N_GROUPS, d), F32)], axis=0)
    br = jnp.concatenate([b_router_expert[l], b_router_group[l],
                          jnp.zeros((ROUTER_ROWS - N_EXPERTS - N_GROUPS,), F32)])[:, None]
    x1, h2, ids, ws, cnt = _post(x, mla_out, d_out, mod, w_out[l].astype(BF16), norm2_g[l][None], wr, br)

    tmx = EXPERT_TM
    counts = cnt[:, 0].astype(jnp.int32)
    padded = ((counts + tmx - 1) // tmx) * tmx
    pends = jnp.cumsum(padded)
    pstarts = pends - padded
    rows = t * TOP_K + N_EXPERTS * tmx
    nb = rows // tmx
    block_row = jnp.arange(nb, dtype=jnp.int32) * tmx
    block_expert = jnp.minimum(jnp.sum((pends[None, :] <= block_row[:, None]).astype(jnp.int32), axis=1),
                               N_EXPERTS - 1).astype(jnp.int32)
    nb_used = (pends[-1:] // tmx).astype(jnp.int32)
    pad_start = jnp.concatenate([pstarts + counts, pends[-1:]]).astype(jnp.int32)

    ids_t = ids.transpose(1, 0, 2).reshape(4, t)
    dest0 = (pstarts[ids_t[0]] + ids_t[2]).astype(jnp.int32)
    dest1 = (pstarts[ids_t[1]] + ids_t[3]).astype(jnp.int32)
    w_tok = ws.transpose(0, 2, 1).reshape(t, TOP_K)

    xs = _dispatch(dest0, dest1, pad_start, h2, rows)
    ys = _experts(block_expert, nb_used, xs, w_expert_gate[l].astype(BF16), w_expert_up[l].astype(BF16),
                  w_expert_down[l].astype(BF16))
    out = _combine(dest0, dest1, x1.reshape(t, d), w_tok, mod, final_norm_g, ys, s)
    return out.reshape(b, s, d)


def kernel(x, c, w_ada, b_ada, norm1_g, w_in, q_a_norm_g, w_q_up, kv_a_norm_g, w_kv_up, lambda_q1, lambda_k1,
           lambda_q2, lambda_k2, subln_g, w_out, norm2_g, w_router_group, b_router_group, w_router_expert,
           b_router_expert, w_expert_gate, w_expert_up, w_expert_down, final_norm_g):
    b, s, d = x.shape
    depth = w_ada.shape[0]
    assert depth == 1, "the final RMSNorm is fused into the last layer's combine step"
    cos_t, sin_t = _rope_tables(s)
    mod = _ada(c, w_ada[0], b_ada[0]).reshape(b, 6, d)
    return _layer(0, x, mod, norm1_g, w_in, q_a_norm_g, w_q_up, kv_a_norm_g, w_kv_up, lambda_q1, lambda_k1,
                  lambda_q2, lambda_k2, subln_g, w_out, norm2_g, w_router_group, b_router_group,
                  w_router_expert, b_router_expert, w_expert_gate, w_expert_up, w_expert_down,
                  final_norm_g[None], cos_t, sin_t)
```

```python
import functools
import math

import jax
import jax.numpy as jnp
from jax import lax
from jax.experimental import pallas as pl
from jax.experimental.pallas import tpu as pltpu

D_MODEL = 1024
ROPE_THETA = 10000.0
EPS = 1e-6

MLA_HEADS = 4
MLA_Q_RANK = 256
MLA_KV_RANK = 256
MLA_NOPE_DIM = 128
MLA_ROPE_DIM = 64
MLA_V_DIM = 128
MLA_QK_PAD = 256

DIFF_HEADS = 4
DIFF_QK_DIM = 64
DIFF_V_DIM = 128

N_GROUPS = 4
EXPERTS_PER_GROUP = 8
N_EXPERTS = 32
TOP_K = 2
D_EXPERT = 512

LANES = 128
ROW_CHUNKS = D_MODEL // LANES
ROUTER_ROWS = 40

VMEM_LIMIT = 56 * 1024 * 1024

PRE_TM = 512
ATT_TQ = 512
ATT_TK = 1024
ATT_SLOTS = 3
ATT_HEADS_PER_STEP = 2
LOG2E = math.log2(math.e)
POST_TM = 512
DISPATCH_TM = 512
EXPERT_TM = 512
COMBINE_TM = 256

F32 = jnp.float32
BF16 = jnp.bfloat16
NEG_INF = float("-inf")


def _cparams(sem):
    return pltpu.CompilerParams(dimension_semantics=sem, vmem_limit_bytes=VMEM_LIMIT)


def _ada_kernel(c_ref, w_ref, b_ref, o_ref):
    c = c_ref[...]
    sc = c / (1.0 + jnp.exp(-c))
    o_ref[...] = jnp.dot(sc, w_ref[...], preferred_element_type=F32,
                         precision=lax.Precision.HIGHEST) + b_ref[...]


def _ada(c, w_ada, b_ada):
    b, d = c.shape
    n = w_ada.shape[1]
    tn = 1536
    return pl.pallas_call(
        _ada_kernel,
        out_shape=jax.ShapeDtypeStruct((b, n), F32),
        grid=(n // tn,),
        in_specs=[pl.BlockSpec((b, d), lambda j: (0, 0)),
                  pl.BlockSpec((d, tn), lambda j: (0, j)),
                  pl.BlockSpec((1, tn), lambda j: (0, j))],
        out_specs=pl.BlockSpec((b, tn), lambda j: (0, j)),
        compiler_params=_cparams(("parallel",)),
        name="ada",
    )(c, w_ada, b_ada.reshape(1, n))


def _rms(x, g):
    return x * lax.rsqrt(jnp.mean(x * x, axis=-1, keepdims=True) + EPS) * g


def _token_tile(ref, t):
    return ref.at[pl.ds(pl.multiple_of(t * ROW_CHUNKS, ROW_CHUNKS), ROW_CHUNKS), :]


def _token_tiles(ref, t, n):
    return ref.at[pl.ds(pl.multiple_of(t * ROW_CHUNKS, ROW_CHUNKS), n * ROW_CHUNKS), :]


def _load_token_rows(ref):
    tm = ref.shape[0] // ROW_CHUNKS
    return [ref[pl.ds(c, tm, stride=ROW_CHUNKS), :] for c in range(ROW_CHUNKS)]


def _store_token_rows(ref, val):
    tm = ref.shape[0] // ROW_CHUNKS
    for c in range(ROW_CHUNKS):
        ref[pl.ds(c, tm, stride=ROW_CHUNKS), :] = val[:, c * LANES:(c + 1) * LANES]


def _rope_tile(x, cos, sin_signed, first_half):
    fwd = pltpu.roll(x, 96, axis=1)
    bwd = pltpu.roll(x, 32, axis=1)
    return x * cos + jnp.where(first_half, fwd, bwd) * sin_signed


def _pre_kernel(x_ref, mod_ref, g1_ref, win_ref, gq_ref, wq_ref, gkv_ref, wkv_ref, cos_ref, sin_ref,
                qm_ref, km_ref, vm_ref, dq_ref, dk_ref, dv_ref):
    x = x_ref[0]
    shift1 = mod_ref[0, 0:1, :]
    scale1 = mod_ref[0, 1:2, :]
    h = _rms(x, g1_ref[...]) * (1.0 + scale1) + shift1
    z = jnp.dot(h.astype(BF16), win_ref[...], preferred_element_type=F32)

    cos = cos_ref[...]
    sin = sin_ref[...]
    lane = lax.broadcasted_iota(jnp.int32, cos.shape, 1)
    first_half = (lane % 64) < 32

    cq = _rms(z[:, 0:256], gq_ref[...])
    q = jnp.dot(cq.astype(BF16), wq_ref[...], preferred_element_type=F32)
    ckv = _rms(z[:, 256:512], gkv_ref[...])
    kv = jnp.dot(ckv.astype(BF16), wkv_ref[...], preferred_element_type=F32)
    kpe = _rope_tile(z[:, 2048:2176], cos, sin, first_half).astype(BF16)

    q_scale = LOG2E * (MLA_NOPE_DIM + MLA_ROPE_DIM) ** -0.5
    for hd in range(MLA_HEADS):
        qn = q[:, hd * 128:(hd + 1) * 128] * q_scale
        qp = _rope_tile(q[:, 512 + hd * 128:512 + (hd + 1) * 128], cos, sin, first_half) * q_scale
        qm_ref[0, hd, :, 0:128] = qn.astype(BF16)
        qm_ref[0, hd, :, 128:256] = qp.astype(BF16)
        km_ref[0, hd, :, 0:128] = kv[:, hd * 128:(hd + 1) * 128].astype(BF16)
        km_ref[0, hd, :, 128:256] = kpe
        vm_ref[0, hd] = kv[:, 512 + hd * 128:512 + (hd + 1) * 128].astype(BF16)

    d_scale = LOG2E * DIFF_QK_DIM ** -0.5
    for hd in range(DIFF_HEADS):
        dq = _rope_tile(z[:, 512 + hd * 128:512 + (hd + 1) * 128], cos, sin, first_half) * d_scale
        dk = _rope_tile(z[:, 1024 + hd * 128:1024 + (hd + 1) * 128], cos, sin, first_half)
        dq_ref[0, hd] = dq.astype(BF16)
        dk_ref[0, hd] = dk.astype(BF16)
        dv_ref[0, hd] = z[:, 1536 + hd * 128:1536 + (hd + 1) * 128].astype(BF16)


def _pre(x, mod, g1, win, gq, wq, gkv, wkv, cos_t, sin_t):
    b, s, d = x.shape
    tm = PRE_TM
    full = lambda a: pl.BlockSpec(a.shape, lambda bi, i: (0,) * a.ndim)
    head_out = lambda w: pl.BlockSpec((1, 4, tm, w), lambda bi, i: (bi, 0, i, 0))
    head_shape = lambda w: jax.ShapeDtypeStruct((b, 4, s, w), BF16)
    return pl.pallas_call(
        _pre_kernel,
        out_shape=(head_shape(256), head_shape(256), head_shape(128),
                   head_shape(128), head_shape(128), head_shape(128)),
        grid=(b, s // tm),
        in_specs=[pl.BlockSpec((1, tm, d), lambda bi, i: (bi, i, 0)),
                  pl.BlockSpec((1, 6, d), lambda bi, i: (bi, 0, 0)),
                  full(g1), full(win), full(gq), full(wq), full(gkv), full(wkv),
                  pl.BlockSpec((tm, LANES), lambda bi, i: (i, 0)),
                  pl.BlockSpec((tm, LANES), lambda bi, i: (i, 0))],
        out_specs=(head_out(256), head_out(256), head_out(128),
                   head_out(128), head_out(128), head_out(128)),
        compiler_params=_cparams(("parallel", "parallel")),
        name="pre",
    )(x, mod, g1, win, gq, wq, gkv, wkv, cos_t, sin_t)


def _softmax_pv(s, vx, m, accx):
    m_new = jnp.maximum(m, jnp.max(s, axis=-1, keepdims=True))
    alpha = jnp.exp2(m - m_new)
    p = jnp.exp2(s - m_new)
    accx_new = alpha * accx + jnp.dot(p.astype(BF16), vx, preferred_element_type=F32)
    return m_new, accx_new


def _attend(streams, n_chunks, s_ref):
    steps = [(si, c) for si in range(len(streams)) for c in range(n_chunks)]
    ahead = ATT_SLOTS - 1
    ones = jnp.ones((ATT_TK, LANES), BF16)

    def scores_into(idx):
        si, c = steps[idx]
        q, k_chunk, _ = streams[si]
        s_ref[idx % ATT_SLOTS] = lax.dot_general(q, k_chunk(c), (((1,), (1,)), ((), ())),
                                                 preferred_element_type=F32)

    for idx in range(min(ahead, len(steps))):
        scores_into(idx)
    outs = []
    state = None
    for idx, (si, c) in enumerate(steps):
        q, _, v_chunk = streams[si]
        if c == 0:
            state = (jnp.full((q.shape[0], 1), NEG_INF, F32), jnp.zeros((q.shape[0], 2 * LANES), F32))
        if idx + ahead < len(steps):
            scores_into(idx + ahead)
        vx = jnp.concatenate([v_chunk(c), ones], axis=-1)
        state = _softmax_pv(s_ref[idx % ATT_SLOTS], vx, *state)
        if c == n_chunks - 1:
            outs.append(state[1])
    return outs


def _chunk_of(ref, head):
    return lambda c: ref[0, head, c * ATT_TK:(c + 1) * ATT_TK, :]


def _normalise(accx):
    return accx[:, :LANES] / accx[:, LANES:]


def _mla_kernel(q_ref, k_ref, v_ref, o_ref, s_ref):
    n_chunks = k_ref.shape[2] // ATT_TK
    streams = [(q_ref[0, h], _chunk_of(k_ref, h), _chunk_of(v_ref, h)) for h in range(ATT_HEADS_PER_STEP)]
    for h, accx in enumerate(_attend(streams, n_chunks, s_ref)):
        o_ref[0, :, h * MLA_V_DIM:(h + 1) * MLA_V_DIM] = _normalise(accx).astype(o_ref.dtype)


def _mla_attn(q, k, v):
    b, h, s, _ = q.shape
    tq = ATT_TQ
    hp = ATT_HEADS_PER_STEP
    return pl.pallas_call(
        _mla_kernel,
        out_shape=jax.ShapeDtypeStruct((b, s, h * MLA_V_DIM), BF16),
        grid=(b, h // hp, s // tq),
        in_specs=[pl.BlockSpec((1, hp, tq, MLA_QK_PAD), lambda bi, hi, i: (bi, hi, i, 0)),
                  pl.BlockSpec((1, hp, s, MLA_QK_PAD), lambda bi, hi, i: (bi, hi, 0, 0)),
                  pl.BlockSpec((1, hp, s, MLA_V_DIM), lambda bi, hi, i: (bi, hi, 0, 0))],
        out_specs=pl.BlockSpec((1, tq, hp * MLA_V_DIM), lambda bi, hi, i: (bi, i, hi)),
        scratch_shapes=[pltpu.VMEM((ATT_SLOTS, tq, ATT_TK), F32)],
        compiler_params=_cparams(("parallel", "parallel", "parallel")),
        name="mla_attn",
    )(q, k, v)


def _diff_kernel(lam_ref, g_ref, q_ref, k_ref, v_ref, o_ref, s_ref, *, out_scale, lambda_init):
    n_chunks = k_ref.shape[2] // ATT_TK
    streams = []
    for h in range(ATT_HEADS_PER_STEP):
        q = q_ref[0, h]
        lane = lax.broadcasted_iota(jnp.int32, q.shape, 1)
        zero = jnp.zeros_like(q)
        q1 = jnp.where(lane < DIFF_QK_DIM, q, zero)
        q2 = jnp.where(lane < DIFF_QK_DIM, zero, q)
        streams += [(q1, _chunk_of(k_ref, h), _chunk_of(v_ref, h)), (q2, _chunk_of(k_ref, h), _chunk_of(v_ref, h))]
    acc = _attend(streams, n_chunks, s_ref)

    lv = lam_ref[...]
    lam = (jnp.exp(jnp.sum(lv[0:1] * lv[1:2], axis=-1, keepdims=True))
           - jnp.exp(jnp.sum(lv[2:3] * lv[3:4], axis=-1, keepdims=True)) + lambda_init)
    for h in range(ATT_HEADS_PER_STEP):
        o = _normalise(acc[2 * h]) - lam * _normalise(acc[2 * h + 1])
        o_ref[0, :, h * DIFF_V_DIM:(h + 1) * DIFF_V_DIM] = (_rms(o, g_ref[...]) * out_scale).astype(o_ref.dtype)


def _diff_attn(lam_vecs, subln_g, q, k, v, lambda_init):
    b, h, s, _ = q.shape
    tq = ATT_TQ
    hp = ATT_HEADS_PER_STEP
    kern = functools.partial(_diff_kernel, out_scale=1.0 - lambda_init, lambda_init=lambda_init)
    return pl.pallas_call(
        kern,
        out_shape=jax.ShapeDtypeStruct((b, s, h * DIFF_V_DIM), BF16),
        grid=(b, h // hp, s // tq),
        in_specs=[pl.BlockSpec((4, DIFF_QK_DIM), lambda bi, hi, i: (0, 0)),
                  pl.BlockSpec((1, DIFF_V_DIM), lambda bi, hi, i: (0, 0)),
                  pl.BlockSpec((1, hp, tq, 128), lambda bi, hi, i: (bi, hi, i, 0)),
                  pl.BlockSpec((1, hp, s, 128), lambda bi, hi, i: (bi, hi, 0, 0)),
                  pl.BlockSpec((1, hp, s, DIFF_V_DIM), lambda bi, hi, i: (bi, hi, 0, 0))],
        out_specs=pl.BlockSpec((1, tq, hp * DIFF_V_DIM), lambda bi, hi, i: (bi, i, hi)),
        scratch_shapes=[pltpu.VMEM((ATT_SLOTS, tq, ATT_TK), F32)],
        compiler_params=_cparams(("parallel", "parallel", "parallel")),
        name="diff_attn",
    )(lam_vecs, subln_g, q, k, v)


def _post_kernel(x_ref, mla_ref, dif_ref, mod_ref, wo_ref, g2_ref, wr_ref, br_ref,
                 x1_ref, h2_ref, ids_ref, ws_ref, cnt_ref, base_ref):
    first = jnp.logical_and(pl.program_id(0) == 0, pl.program_id(1) == 0)

    @pl.when(first)
    def _():
        base_ref[...] = jnp.zeros_like(base_ref)

    tm = x_ref.shape[1]
    gate1 = mod_ref[0, 2:3, :]
    shift2 = mod_ref[0, 3:4, :]
    scale2 = mod_ref[0, 4:5, :]
    mixin = jnp.concatenate([mla_ref[0], dif_ref[0]], axis=-1)
    mix = jnp.dot(mixin, wo_ref[...], preferred_element_type=F32)
    x1 = x_ref[0] + gate1 * mix
    x1_ref[0] = x1
    h2 = _rms(x1, g2_ref[...]) * (1.0 + scale2) + shift2
    _store_token_rows(h2_ref, h2)

    logits = lax.dot_general(wr_ref[...], h2, (((1,), (1,)), ((), ())), preferred_element_type=F32,
                             precision=lax.Precision.HIGHEST) + br_ref[...]
    el = logits[0:N_EXPERTS]
    gl = logits[N_EXPERTS:N_EXPERTS + N_GROUPS]

    grow = lax.broadcasted_iota(jnp.int32, gl.shape, 0).astype(F32)
    gmax = jnp.max(gl, axis=0, keepdims=True)
    gidx = jnp.min(jnp.where(gl == gmax, grow, float(N_GROUPS)), axis=0, keepdims=True)
    gprob = 1.0 / jnp.sum(jnp.exp(gl - gmax), axis=0, keepdims=True)

    erow_i = lax.broadcasted_iota(jnp.int32, el.shape, 0)
    erow = erow_i.astype(F32)
    egrp = (erow_i // EXPERTS_PER_GROUP).astype(F32)
    masked = jnp.where(egrp == gidx, el, NEG_INF)
    l1 = jnp.max(masked, axis=0, keepdims=True)
    i1 = jnp.min(jnp.where(masked == l1, erow, float(N_EXPERTS)), axis=0, keepdims=True)
    oh1 = erow == i1
    masked2 = jnp.where(oh1, NEG_INF, masked)
    l2 = jnp.max(masked2, axis=0, keepdims=True)
    i2 = jnp.min(jnp.where(masked2 == l2, erow, float(N_EXPERTS)), axis=0, keepdims=True)
    oh2 = erow == i2
    e21 = jnp.exp(l2 - l1)
    w1 = gprob * (1.0 / (1.0 + e21))
    w2 = gprob * (e21 / (1.0 + e21))

    oh = jnp.where(jnp.logical_or(oh1, oh2), 1.0, 0.0)
    tr = lax.broadcasted_iota(jnp.int32, (tm, tm), 0)
    tc = lax.broadcasted_iota(jnp.int32, (tm, tm), 1)
    upper = jnp.where(tr < tc, 1.0, 0.0).astype(BF16)
    pos = base_ref[...][:, 0:1] + jnp.dot(oh.astype(BF16), upper, preferred_element_type=F32)
    r1 = jnp.sum(jnp.where(oh1, pos, 0.0), axis=0, keepdims=True)
    r2 = jnp.sum(jnp.where(oh2, pos, 0.0), axis=0, keepdims=True)
    base_new = base_ref[...] + jnp.sum(oh, axis=1, keepdims=True)
    base_ref[...] = base_new
    cnt_ref[...] = base_new

    ids_ref[0] = jnp.concatenate([i1, i2, r1, r2], axis=0).astype(jnp.int32)
    ws_ref[0] = jnp.concatenate([w1, w2], axis=0)


def _post(x, mla_out, d_out, mod, wo, g2, wr, br):
    b, s, d = x.shape
    tm = POST_TM
    nt = s // tm
    full = lambda a: pl.BlockSpec(a.shape, lambda bi, i: (0,) * a.ndim)
    tok = lambda w: pl.BlockSpec((1, tm, w), lambda bi, i: (bi, i, 0))
    return pl.pallas_call(
        _post_kernel,
        out_shape=(jax.ShapeDtypeStruct((b, s, d), F32),
                   jax.ShapeDtypeStruct((b * s * ROW_CHUNKS, LANES), F32),
                   jax.ShapeDtypeStruct((b * nt, 4, tm), jnp.int32),
                   jax.ShapeDtypeStruct((b * nt, 2, tm), F32),
                   jax.ShapeDtypeStruct((N_EXPERTS, LANES), F32)),
        grid=(b, nt),
        in_specs=[tok(d), tok(mla_out.shape[2]), tok(d_out.shape[2]),
                  pl.BlockSpec((1, 6, d), lambda bi, i: (bi, 0, 0)),
                  full(wo), full(g2), full(wr), full(br)],
        out_specs=(tok(d),
                   pl.BlockSpec((tm * ROW_CHUNKS, LANES), lambda bi, i: (bi * nt + i, 0)),
                   pl.BlockSpec((1, 4, tm), lambda bi, i: (bi * nt + i, 0, 0)),
                   pl.BlockSpec((1, 2, tm), lambda bi, i: (bi * nt + i, 0, 0)),
                   pl.BlockSpec((N_EXPERTS, LANES), lambda bi, i: (0, 0))),
        scratch_shapes=[pltpu.VMEM((N_EXPERTS, LANES), F32)],
        compiler_params=_cparams(("arbitrary", "arbitrary")),
        name="post",
    )(x, mla_out, d_out, mod, wo, g2, wr, br)


def _dispatch_kernel(d0_ref, d1_ref, pad_ref, h_ref, xs_ref, zero_ref, sem):
    tm = h_ref.shape[0] // ROW_CHUNKS
    i = pl.program_id(0)

    @pl.when(i == 0)
    def _():
        zero_ref[...] = jnp.zeros_like(zero_ref)
        for e in range(N_EXPERTS):
            pltpu.make_async_copy(zero_ref, _token_tiles(xs_ref, pad_ref[e], EXPERT_TM), sem).start()
        for e in range(N_EXPERTS):
            pltpu.make_async_copy(zero_ref, _token_tiles(xs_ref, pad_ref[e], EXPERT_TM), sem).wait()
        n_rows = xs_ref.shape[0] // ROW_CHUNKS
        for e in range(N_EXPERTS):
            start = pad_ref[N_EXPERTS] + e * EXPERT_TM

            @pl.when(start < n_rows)
            def _():
                pltpu.make_async_copy(zero_ref, _token_tiles(xs_ref, start, EXPERT_TM), sem).start()

        for e in range(N_EXPERTS):
            start = pad_ref[N_EXPERTS] + e * EXPERT_TM

            @pl.when(start < n_rows)
            def _():
                pltpu.make_async_copy(zero_ref, _token_tiles(xs_ref, start, EXPERT_TM), sem).wait()

    base = i * tm

    def issue(r, carry):
        src = _token_tile(h_ref, r)
        pltpu.make_async_copy(src, _token_tile(xs_ref, d0_ref[base + r]), sem).start()
        pltpu.make_async_copy(src, _token_tile(xs_ref, d1_ref[base + r]), sem).start()
        return carry

    lax.fori_loop(0, tm, issue, 0, unroll=8)

    def drain(r, carry):
        src = _token_tile(h_ref, r)
        pltpu.make_async_copy(src, _token_tile(xs_ref, 0), sem).wait()
        pltpu.make_async_copy(src, _token_tile(xs_ref, 0), sem).wait()
        return carry

    lax.fori_loop(0, tm, drain, 0, unroll=8)


def _dispatch(dest0, dest1, pad_start, h2, rows):
    t = h2.shape[0] // ROW_CHUNKS
    tm = DISPATCH_TM
    return pl.pallas_call(
        _dispatch_kernel,
        out_shape=jax.ShapeDtypeStruct((rows * ROW_CHUNKS, LANES), F32),
        grid_spec=pltpu.PrefetchScalarGridSpec(
            num_scalar_prefetch=3,
            grid=(t // tm,),
            in_specs=[pl.BlockSpec((tm * ROW_CHUNKS, LANES), lambda i, *_: (i, 0))],
            out_specs=pl.BlockSpec(memory_space=pl.ANY),
            scratch_shapes=[pltpu.VMEM((EXPERT_TM * ROW_CHUNKS, LANES), F32), pltpu.SemaphoreType.DMA]),
        compiler_params=_cparams(("arbitrary",)),
        name="dispatch",
    )(dest0, dest1, pad_start, h2)


def _expert_kernel(be_ref, nbu_ref, x_ref, wg_ref, wu_ref, wd_ref, y_ref):
    @pl.when(pl.program_id(0) < nbu_ref[0])
    def _():
        x = jnp.concatenate(_load_token_rows(x_ref), axis=-1).astype(BF16)
        g = jnp.dot(x, wg_ref[0], preferred_element_type=F32)
        u = jnp.dot(x, wu_ref[0], preferred_element_type=F32)
        a = (g / (1.0 + jnp.exp(-g))) * u
        y = jnp.dot(a.astype(BF16), wd_ref[0], preferred_element_type=F32)
        _store_token_rows(y_ref, y)

    @pl.when(pl.program_id(0) >= nbu_ref[0])
    def _():
        y_ref[...] = jnp.zeros_like(y_ref)


def _experts(block_expert, nb_used, xs, wg, wu, wd):
    rows = xs.shape[0] // ROW_CHUNKS
    d = D_MODEL
    tm = EXPERT_TM
    nb = rows // tm
    row_map = lambda j, be, nbu: (jnp.minimum(j, nbu[0] - 1), 0)
    w_map = lambda j, be, nbu: (be[j], 0, 0)
    return pl.pallas_call(
        _expert_kernel,
        out_shape=jax.ShapeDtypeStruct((rows * ROW_CHUNKS, LANES), F32),
        grid_spec=pltpu.PrefetchScalarGridSpec(
            num_scalar_prefetch=2,
            grid=(nb,),
            in_specs=[pl.BlockSpec((tm * ROW_CHUNKS, LANES), row_map),
                      pl.BlockSpec((1, d, D_EXPERT), w_map),
                      pl.BlockSpec((1, d, D_EXPERT), w_map),
                      pl.BlockSpec((1, D_EXPERT, d), w_map)],
            out_specs=pl.BlockSpec((tm * ROW_CHUNKS, LANES), lambda j, be, nbu: (j, 0))),
        compiler_params=_cparams(("arbitrary",)),
        name="experts",
    )(block_expert, nb_used, xs, wg, wu, wd)


def _combine_kernel(d0_ref, d1_ref, x1_ref, w_ref, mod_ref, gf_ref, ys_ref, o_ref, y0_ref, y1_ref, sems):
    tm = x1_ref.shape[0]
    i = pl.program_id(0)
    n = pl.num_programs(0)

    def gather(step, slot, start):
        base = step * tm

        def one(r, carry):
            c0 = pltpu.make_async_copy(_token_tile(ys_ref, d0_ref[base + r] if start else 0),
                                       _token_tile(y0_ref.at[slot], r), sems.at[slot])
            c1 = pltpu.make_async_copy(_token_tile(ys_ref, d1_ref[base + r] if start else 0),
                                       _token_tile(y1_ref.at[slot], r), sems.at[slot])
            if start:
                c0.start()
                c1.start()
            else:
                c0.wait()
                c1.wait()
            return carry

        lax.fori_loop(0, tm, one, 0, unroll=8)

    slot = i % 2

    @pl.when(i == 0)
    def _():
        gather(0, 0, True)

    @pl.when(i + 1 < n)
    def _():
        gather(i + 1, 1 - slot, True)

    gather(i, slot, False)

    gate2 = mod_ref[0, 5:6, :]
    w = w_ref[...]
    w0 = w[:, 0:1]
    w1 = w[:, 1:2]
    y0 = _load_token_rows(y0_ref.at[slot])
    y1 = _load_token_rows(y1_ref.at[slot])
    moe = jnp.concatenate([w0 * a + w1 * b for a, b in zip(y0, y1)], axis=-1)
    x2 = x1_ref[...] + gate2 * moe
    o_ref[...] = _rms(x2, gf_ref[...])


def _combine(dest0, dest1, x1, w_tok, mod, gf, ys, s_len):
    t, d = x1.shape
    tm = COMBINE_TM
    per_batch = s_len // tm
    return pl.pallas_call(
        _combine_kernel,
        out_shape=jax.ShapeDtypeStruct((t, d), F32),
        grid_spec=pltpu.PrefetchScalarGridSpec(
            num_scalar_prefetch=2,
            grid=(t // tm,),
            in_specs=[pl.BlockSpec((tm, d), lambda i, *_: (i, 0)),
                      pl.BlockSpec((tm, TOP_K), lambda i, *_: (i, 0)),
                      pl.BlockSpec((1, 6, d), lambda i, *_: (i // per_batch, 0, 0)),
                      pl.BlockSpec((1, d), lambda i, *_: (0, 0)),
                      pl.BlockSpec(memory_space=pl.ANY)],
            out_specs=pl.BlockSpec((tm, d), lambda i, *_: (i, 0)),
            scratch_shapes=[pltpu.VMEM((2, tm * ROW_CHUNKS, LANES), F32),
                            pltpu.VMEM((2, tm * ROW_CHUNKS, LANES), F32),
                            pltpu.SemaphoreType.DMA((2,))]),
        compiler_params=_cparams(("arbitrary",)),
        name="combine",
    )(dest0, dest1, x1, w_tok, mod, gf, ys)


def _rope_tables(s_len):
    half = MLA_ROPE_DIM // 2
    inv = ROPE_THETA ** (-jnp.arange(0, MLA_ROPE_DIM, 2, dtype=F32) / MLA_ROPE_DIM)
    ang = jnp.arange(s_len, dtype=jnp.int32).astype(F32)[:, None] * inv[None, :]
    cos, sin = jnp.cos(ang), jnp.sin(ang)
    reps = LANES // MLA_ROPE_DIM
    cos_t = jnp.tile(jnp.concatenate([cos, cos], axis=1), (1, reps))
    sin_t = jnp.tile(jnp.concatenate([-sin, sin], axis=1), (1, reps))
    del half
    return cos_t, sin_t


def _layer(l, x, mod, norm1_g, w_in, q_a_norm_g, w_q_up, kv_a_norm_g, w_kv_up, lambda_q1, lambda_k1,
           lambda_q2, lambda_k2, subln_g, w_out, norm2_g, w_router_group, b_router_group, w_router_expert,
           b_router_expert, w_expert_gate, w_expert_up, w_expert_down, final_norm_g, cos_t, sin_t):
    b, s, d = x.shape
    t = b * s

    wi = w_in[l]
    win = jnp.concatenate([wi[:, 0:512], wi[:, 576:2112], wi[:, 512:576], jnp.zeros((d, 64), F32)],
                          axis=1).astype(BF16)
    wq4 = w_q_up[l].reshape(MLA_Q_RANK, MLA_HEADS, MLA_NOPE_DIM + MLA_ROPE_DIM)
    wq_pe = jnp.pad(wq4[:, :, MLA_NOPE_DIM:], ((0, 0), (0, 0), (0, 64)))
    wq = jnp.concatenate([wq4[:, :, :MLA_NOPE_DIM].reshape(MLA_Q_RANK, 512),
                          wq_pe.reshape(MLA_Q_RANK, 512)], axis=1).astype(BF16)
    wkv4 = w_kv_up[l].reshape(MLA_KV_RANK, MLA_HEADS, MLA_NOPE_DIM + MLA_V_DIM)
    wkv = jnp.concatenate([wkv4[:, :, :MLA_NOPE_DIM].reshape(MLA_KV_RANK, 512),
                           wkv4[:, :, MLA_NOPE_DIM:].reshape(MLA_KV_RANK, 512)], axis=1).astype(BF16)

    qm, km, vm, dq, dk, dv = _pre(x, mod, norm1_g[l][None], win, q_a_norm_g[l][None], wq,
                                  kv_a_norm_g[l][None], wkv, cos_t, sin_t)

    mla_out = _mla_attn(qm, km, vm)
    lambda_init = 0.8 - 0.6 * math.exp(-0.3 * l)
    lam_vecs = jnp.stack([lambda_q1[l], lambda_k1[l], lambda_q2[l], lambda_k2[l]]).astype(F32)
    d_out = _diff_attn(lam_vecs, subln_g[l][None], dq, dk, dv, lambda_init)

    wr = jnp.concatenate([w_router_expert[l].T, w_router_group[l].T,
                          jnp.zeros((ROUTER_ROWS - N_EXPERTS - N_GROUPS, d), F32)], axis=0)
    br = jnp.concatenate([b_router_expert[l], b_router_group[l],
                          jnp.zeros((ROUTER_ROWS - N_EXPERTS - N_GROUPS,), F32)])[:, None]
    x1, h2, ids, ws, cnt = _post(x, mla_out, d_out, mod, w_out[l].astype(BF16), norm2_g[l][None], wr, br)

    tmx = EXPERT_TM
    counts = cnt[:, 0].astype(jnp.int32)
    padded = ((counts + tmx - 1) // tmx) * tmx
    pends = jnp.cumsum(padded)
    pstarts = pends - padded
    rows = t * TOP_K + N_EXPERTS * tmx
    nb = rows // tmx
    block_row = jnp.arange(nb, dtype=jnp.int32) * tmx
    block_expert = jnp.minimum(jnp.sum((pends[None, :] <= block_row[:, None]).astype(jnp.int32), axis=1),
                               N_EXPERTS - 1).astype(jnp.int32)
    nb_used = (pends[-1:] // tmx).astype(jnp.int32)
    pad_start = jnp.concatenate([pstarts + counts, pends[-1:]]).astype(jnp.int32)

    ids_t = ids.transpose(1, 0, 2).reshape(4, t)
    dest0 = (pstarts[ids_t[0]] + ids_t[2]).astype(jnp.int32)
    dest1 = (pstarts[ids_t[1]] + ids_t[3]).astype(jnp.int32)
    w_tok = ws.transpose(0, 2, 1).reshape(t, TOP_K)

    xs = _dispatch(dest0, dest1, pad_start, h2, rows)
    ys = _experts(block_expert, nb_used, xs, w_expert_gate[l].astype(BF16), w_expert_up[l].astype(BF16),
                  w_expert_down[l].astype(BF16))
    out = _combine(dest0, dest1, x1.reshape(t, d), w_tok, mod, final_norm_g, ys, s)
    return out.reshape(b, s, d)


def kernel(x, c, w_ada, b_ada, norm1_g, w_in, q_a_norm_g, w_q_up, kv_a_norm_g, w_kv_up, lambda_q1, lambda_k1,
           lambda_q2, lambda_k2, subln_g, w_out, norm2_g, w_router_group, b_router_group, w_router_expert,
           b_router_expert, w_expert_gate, w_expert_up, w_expert_down, final_norm_g):
    b, s, d = x.shape
    depth = w_ada.shape[0]
    assert depth == 1, "the final RMSNorm is fused into the last layer's combine step"
    cos_t, sin_t = _rope_tables(s)
    mod = _ada(c, w_ada[0], b_ada[0]).reshape(b, 6, d)
    return _layer(0, x, mod, norm1_g, w_in, q_a_norm_g, w_q_up, kv_a_norm_g, w_kv_up, lambda_q1, lambda_k1,
                  lambda_q2, lambda_k2, subln_g, w_out, norm2_g, w_router_group, b_router_group,
                  w_router_expert, b_router_expert, w_expert_gate, w_expert_up, w_expert_down,
                  final_norm_g[None], cos_t, sin_t)
```

```python
import functools
import math

import jax
import jax.numpy as jnp
from jax import lax
from jax.experimental import pallas as pl
from jax.experimental.pallas import tpu as pltpu

D_MODEL = 1024
ROPE_THETA = 10000.0
EPS = 1e-6

MLA_HEADS = 4
MLA_Q_RANK = 256
MLA_KV_RANK = 256
MLA_NOPE_DIM = 128
MLA_ROPE_DIM = 64
MLA_V_DIM = 128
MLA_QK_PAD = 256

DIFF_HEADS = 4
DIFF_QK_DIM = 64
DIFF_V_DIM = 128

N_GROUPS = 4
EXPERTS_PER_GROUP = 8
N_EXPERTS = 32
TOP_K = 2
D_EXPERT = 512

LANES = 128
ROW_CHUNKS = D_MODEL // LANES
ROUTER_ROWS = 40

VMEM_LIMIT = 56 * 1024 * 1024

PRE_TM = 512
ATT_TQ = 512
ATT_TK = 1024
ATT_SLOTS = 3
ATT_HEADS_PER_STEP = 2
LOG2E = math.log2(math.e)
POST_TM = 512
DISPATCH_TM = 512
EXPERT_TM = 512
COMBINE_TM = 256

F32 = jnp.float32
BF16 = jnp.bfloat16
NEG_INF = float("-inf")


def _cparams(sem):
    return pltpu.CompilerParams(dimension_semantics=sem, vmem_limit_bytes=VMEM_LIMIT)


def _ada_kernel(c_ref, w_ref, b_ref, o_ref):
    c = c_ref[...]
    sc = c / (1.0 + jnp.exp(-c))
    o_ref[...] = jnp.dot(sc, w_ref[...], preferred_element_type=F32,
                         precision=lax.Precision.HIGHEST) + b_ref[...]


def _ada(c, w_ada, b_ada):
    b, d = c.shape
    n = w_ada.shape[1]
    tn = 1536
    return pl.pallas_call(
        _ada_kernel,
        out_shape=jax.ShapeDtypeStruct((b, n), F32),
        grid=(n // tn,),
        in_specs=[pl.BlockSpec((b, d), lambda j: (0, 0)),
                  pl.BlockSpec((d, tn), lambda j: (0, j)),
                  pl.BlockSpec((1, tn), lambda j: (0, j))],
        out_specs=pl.BlockSpec((b, tn), lambda j: (0, j)),
        compiler_params=_cparams(("parallel",)),
        name="ada",
    )(c, w_ada, b_ada.reshape(1, n))


def _rms(x, g):
    return x * lax.rsqrt(jnp.mean(x * x, axis=-1, keepdims=True) + EPS) * g


def _token_tile(ref, t):
    return ref.at[pl.ds(pl.multiple_of(t * ROW_CHUNKS, ROW_CHUNKS), ROW_CHUNKS), :]


def _token_tiles(ref, t, n):
    return ref.at[pl.ds(pl.multiple_of(t * ROW_CHUNKS, ROW_CHUNKS), n * ROW_CHUNKS), :]


def _load_token_rows(ref):
    tm = ref.shape[0] // ROW_CHUNKS
    return [ref[pl.ds(c, tm, stride=ROW_CHUNKS), :] for c in range(ROW_CHUNKS)]


def _store_token_rows(ref, val):
    tm = ref.shape[0] // ROW_CHUNKS
    for c in range(ROW_CHUNKS):
        ref[pl.ds(c, tm, stride=ROW_CHUNKS), :] = val[:, c * LANES:(c + 1) * LANES]


def _rope_tile(x, cos, sin_signed, first_half):
    fwd = pltpu.roll(x, 96, axis=1)
    bwd = pltpu.roll(x, 32, axis=1)
    return x * cos + jnp.where(first_half, fwd, bwd) * sin_signed


def _pre_kernel(x_ref, mod_ref, g1_ref, win_ref, gq_ref, wq_ref, gkv_ref, wkv_ref, cos_ref, sin_ref,
                qm_ref, km_ref, vm_ref, dq_ref, dk_ref, dv_ref):
    x = x_ref[0]
    shift1 = mod_ref[0, 0:1, :]
    scale1 = mod_ref[0, 1:2, :]
    h = _rms(x, g1_ref[...]) * (1.0 + scale1) + shift1
    z = jnp.dot(h.astype(BF16), win_ref[...], preferred_element_type=F32)

    cos = cos_ref[...]
    sin = sin_ref[...]
    lane = lax.broadcasted_iota(jnp.int32, cos.shape, 1)
    first_half = (lane % 64) < 32

    cq = _rms(z[:, 0:256], gq_ref[...])
    q = jnp.dot(cq.astype(BF16), wq_ref[...], preferred_element_type=F32)
    ckv = _rms(z[:, 256:512], gkv_ref[...])
    kv = jnp.dot(ckv.astype(BF16), wkv_ref[...], preferred_element_type=F32)
    kpe = _rope_tile(z[:, 2048:2176], cos, sin, first_half).astype(BF16)

    q_scale = LOG2E * (MLA_NOPE_DIM + MLA_ROPE_DIM) ** -0.5
    for hd in range(MLA_HEADS):
        qn = q[:, hd * 128:(hd + 1) * 128] * q_scale
        qp = _rope_tile(q[:, 512 + hd * 128:512 + (hd + 1) * 128], cos, sin, first_half) * q_scale
        qm_ref[0, hd, :, 0:128] = qn.astype(BF16)
        qm_ref[0, hd, :, 128:256] = qp.astype(BF16)
        km_ref[0, hd, :, 0:128] = kv[:, hd * 128:(hd + 1) * 128].astype(BF16)
        km_ref[0, hd, :, 128:256] = kpe
        vm_ref[0, hd] = kv[:, 512 + hd * 128:512 + (hd + 1) * 128].astype(BF16)

    d_scale = LOG2E * DIFF_QK_DIM ** -0.5
    for hd in range(DIFF_HEADS):
        dq = _rope_tile(z[:, 512 + hd * 128:512 + (hd + 1) * 128], cos, sin, first_half) * d_scale
        dk = _rope_tile(z[:, 1024 + hd * 128:1024 + (hd + 1) * 128], cos, sin, first_half)
        dq_ref[0, hd] = dq.astype(BF16)
        dk_ref[0, hd] = dk.astype(BF16)
        dv_ref[0, hd] = z[:, 1536 + hd * 128:1536 + (hd + 1) * 128].astype(BF16)


def _pre(x, mod, g1, win, gq, wq, gkv, wkv, cos_t, sin_t):
    b, s, d = x.shape
    tm = PRE_TM
    full = lambda a: pl.BlockSpec(a.shape, lambda bi, i: (0,) * a.ndim)
    head_out = lambda w: pl.BlockSpec((1, 4, tm, w), lambda bi, i: (bi, 0, i, 0))
    head_shape = lambda w: jax.ShapeDtypeStruct((b, 4, s, w), BF16)
    return pl.pallas_call(
        _pre_kernel,
        out_shape=(head_shape(256), head_shape(256), head_shape(128),
                   head_shape(128), head_shape(128), head_shape(128)),
        grid=(b, s // tm),
        in_specs=[pl.BlockSpec((1, tm, d), lambda bi, i: (bi, i, 0)),
                  pl.BlockSpec((1, 6, d), lambda bi, i: (bi, 0, 0)),
                  full(g1), full(win), full(gq), full(wq), full(gkv), full(wkv),
                  pl.BlockSpec((tm, LANES), lambda bi, i: (i, 0)),
                  pl.BlockSpec((tm, LANES), lambda bi, i: (i, 0))],
        out_specs=(head_out(256), head_out(256), head_out(128),
                   head_out(128), head_out(128), head_out(128)),
        compiler_params=_cparams(("parallel", "parallel")),
        name="pre",
    )(x, mod, g1, win, gq, wq, gkv, wkv, cos_t, sin_t)


def _softmax_pv(s, vx, m, accx):
    m_new = jnp.maximum(m, jnp.max(s, axis=-1, keepdims=True))
    alpha = jnp.exp2(m - m_new)
    p = jnp.exp2(s - m_new)
    accx_new = alpha * accx + jnp.dot(p.astype(BF16), vx, preferred_element_type=F32)
    return m_new, accx_new


def _attend(streams, n_chunks, s_ref):
    steps = [(si, c) for si in range(len(streams)) for c in range(n_chunks)]
    ahead = ATT_SLOTS - 1
    ones = jnp.ones((ATT_TK, LANES), BF16)

    def scores_into(idx):
        si, c = steps[idx]
        q, k_chunk, _ = streams[si]
        s_ref[idx % ATT_SLOTS] = lax.dot_general(q, k_chunk(c), (((1,), (1,)), ((), ())),
                                                 preferred_element_type=F32)

    for idx in range(min(ahead, len(steps))):
        scores_into(idx)
    outs = []
    state = None
    for idx, (si, c) in enumerate(steps):
        q, _, v_chunk = streams[si]
        if c == 0:
            state = (jnp.full((q.shape[0], 1), NEG_INF, F32), jnp.zeros((q.shape[0], 2 * LANES), F32))
        if idx + ahead < len(steps):
            scores_into(idx + ahead)
        vx = jnp.concatenate([v_chunk(c), ones], axis=-1)
        state = _softmax_pv(s_ref[idx % ATT_SLOTS], vx, *state)
        if c == n_chunks - 1:
            outs.append(state[1])
    return outs


def _chunk_of(ref, head):
    return lambda c: ref[0, head, c * ATT_TK:(c + 1) * ATT_TK, :]


def _normalise(accx):
    return accx[:, :LANES] / accx[:, LANES:]


def _mla_kernel(q_ref, k_ref, v_ref, o_ref, s_ref):
    n_chunks = k_ref.shape[2] // ATT_TK
    streams = [(q_ref[0, h], _chunk_of(k_ref, h), _chunk_of(v_ref, h)) for h in range(ATT_HEADS_PER_STEP)]
    for h, accx in enumerate(_attend(streams, n_chunks, s_ref)):
        o_ref[0, :, h * MLA_V_DIM:(h + 1) * MLA_V_DIM] = _normalise(accx).astype(o_ref.dtype)


def _mla_attn(q, k, v):
    b, h, s, _ = q.shape
    tq = ATT_TQ
    hp = ATT_HEADS_PER_STEP
    return pl.pallas_call(
        _mla_kernel,
        out_shape=jax.ShapeDtypeStruct((b, s, h * MLA_V_DIM), BF16),
        grid=(b, h // hp, s // tq),
        in_specs=[pl.BlockSpec((1, hp, tq, MLA_QK_PAD), lambda bi, hi, i: (bi, hi, i, 0)),
                  pl.BlockSpec((1, hp, s, MLA_QK_PAD), lambda bi, hi, i: (bi, hi, 0, 0)),
                  pl.BlockSpec((1, hp, s, MLA_V_DIM), lambda bi, hi, i: (bi, hi, 0, 0))],
        out_specs=pl.BlockSpec((1, tq, hp * MLA_V_DIM), lambda bi, hi, i: (bi, i, hi)),
        scratch_shapes=[pltpu.VMEM((ATT_SLOTS, tq, ATT_TK), F32)],
        compiler_params=_cparams(("parallel", "parallel", "parallel")),
        name="mla_attn",
    )(q, k, v)


def _diff_kernel(lam_ref, g_ref, q_ref, k_ref, v_ref, o_ref, s_ref, *, out_scale, lambda_init):
    n_chunks = k_ref.shape[2] // ATT_TK
    streams = []
    for h in range(ATT_HEADS_PER_STEP):
        q = q_ref[0, h]
        lane = lax.broadcasted_iota(jnp.int32, q.shape, 1)
        zero = jnp.zeros_like(q)
        q1 = jnp.where(lane < DIFF_QK_DIM, q, zero)
        q2 = jnp.where(lane < DIFF_QK_DIM, zero, q)
        streams += [(q1, _chunk_of(k_ref, h), _chunk_of(v_ref, h)), (q2, _chunk_of(k_ref, h), _chunk_of(v_ref, h))]
    acc = _attend(streams, n_chunks, s_ref)

    lv = lam_ref[...]
    lam = (jnp.exp(jnp.sum(lv[0:1] * lv[1:2], axis=-1, keepdims=True))
           - jnp.exp(jnp.sum(lv[2:3] * lv[3:4], axis=-1, keepdims=True)) + lambda_init)
    for h in range(ATT_HEADS_PER_STEP):
        o = _normalise(acc[2 * h]) - lam * _normalise(acc[2 * h + 1])
        o_ref[0, :, h * DIFF_V_DIM:(h + 1) * DIFF_V_DIM] = (_rms(o, g_ref[...]) * out_scale).astype(o_ref.dtype)


def _diff_attn(lam_vecs, subln_g, q, k, v, lambda_init):
    b, h, s, _ = q.shape
    tq = ATT_TQ
    hp = ATT_HEADS_PER_STEP
    kern = functools.partial(_diff_kernel, out_scale=1.0 - lambda_init, lambda_init=lambda_init)
    return pl.pallas_call(
        kern,
        out_shape=jax.ShapeDtypeStruct((b, s, h * DIFF_V_DIM), BF16),
        grid=(b, h // hp, s // tq),
        in_specs=[pl.BlockSpec((4, DIFF_QK_DIM), lambda bi, hi, i: (0, 0)),
                  pl.BlockSpec((1, DIFF_V_DIM), lambda bi, hi, i: (0, 0)),
                  pl.BlockSpec((1, hp, tq, 128), lambda bi, hi, i: (bi, hi, i, 0)),
                  pl.BlockSpec((1, hp, s, 128), lambda bi, hi, i: (bi, hi, 0, 0)),
                  pl.BlockSpec((1, hp, s, DIFF_V_DIM), lambda bi, hi, i: (bi, hi, 0, 0))],
        out_specs=pl.BlockSpec((1, tq, hp * DIFF_V_DIM), lambda bi, hi, i: (bi, i, hi)),
        scratch_shapes=[pltpu.VMEM((ATT_SLOTS, tq, ATT_TK), F32)],
        compiler_params=_cparams(("parallel", "parallel", "parallel")),
        name="diff_attn",
    )(lam_vecs, subln_g, q, k, v)


def _post_kernel(x_ref, mla_ref, dif_ref, mod_ref, wo_ref, g2_ref, wr_ref, br_ref,
                 x1_ref, h2_ref, ids_ref, ws_ref, cnt_ref, base_ref):
    first = jnp.logical_and(pl.program_id(0) == 0, pl.program_id(1) == 0)

    @pl.when(first)
    def _():
        base_ref[...] = jnp.zeros_like(base_ref)

    tm = x_ref.shape[1]
    gate1 = mod_ref[0, 2:3, :]
    shift2 = mod_ref[0, 3:4, :]
    scale2 = mod_ref[0, 4:5, :]
    mixin = jnp.concatenate([mla_ref[0], dif_ref[0]], axis=-1)
    mix = jnp.dot(mixin, wo_ref[...], preferred_element_type=F32)
    x1 = x_ref[0] + gate1 * mix
    x1_ref[0] = x1
    h2 = _rms(x1, g2_ref[...]) * (1.0 + scale2) + shift2
    _store_token_rows(h2_ref, h2)

    logits = lax.dot_general(wr_ref[...], h2, (((1,), (1,)), ((), ())), preferred_element_type=F32,
                             precision=lax.Precision.HIGHEST) + br_ref[...]
    el = logits[0:N_EXPERTS]
    gl = logits[N_EXPERTS:N_EXPERTS + N_GROUPS]

    grow = lax.broadcasted_iota(jnp.int32, gl.shape, 0).astype(F32)
    gmax = jnp.max(gl, axis=0, keepdims=True)
    gidx = jnp.min(jnp.where(gl == gmax, grow, float(N_GROUPS)), axis=0, keepdims=True)
    gprob = 1.0 / jnp.sum(jnp.exp(gl - gmax), axis=0, keepdims=True)

    erow_i = lax.broadcasted_iota(jnp.int32, el.shape, 0)
    erow = erow_i.astype(F32)
    egrp = (erow_i // EXPERTS_PER_GROUP).astype(F32)
    masked = jnp.where(egrp == gidx, el, NEG_INF)
    l1 = jnp.max(masked, axis=0, keepdims=True)
    i1 = jnp.min(jnp.where(masked == l1, erow, float(N_EXPERTS)), axis=0, keepdims=True)
    oh1 = erow == i1
    masked2 = jnp.where(oh1, NEG_INF, masked)
    l2 = jnp.max(masked2, axis=0, keepdims=True)
    i2 = jnp.min(jnp.where(masked2 == l2, erow, float(N_EXPERTS)), axis=0, keepdims=True)
    oh2 = erow == i2
    e21 = jnp.exp(l2 - l1)
    w1 = gprob * (1.0 / (1.0 + e21))
    w2 = gprob * (e21 / (1.0 + e21))

    oh = jnp.where(jnp.logical_or(oh1, oh2), 1.0, 0.0)
    tr = lax.broadcasted_iota(jnp.int32, (tm, tm), 0)
    tc = lax.broadcasted_iota(jnp.int32, (tm, tm), 1)
    upper = jnp.where(tr < tc, 1.0, 0.0).astype(BF16)
    pos = base_ref[...][:, 0:1] + jnp.dot(oh.astype(BF16), upper, preferred_element_type=F32)
    r1 = jnp.sum(jnp.where(oh1, pos, 0.0), axis=0, keepdims=True)
    r2 = jnp.sum(jnp.where(oh2, pos, 0.0), axis=0, keepdims=True)
    base_new = base_ref[...] + jnp.sum(oh, axis=1, keepdims=True)
    base_ref[...] = base_new
    cnt_ref[...] = base_new

    ids_ref[0] = jnp.concatenate([i1, i2, r1, r2], axis=0).astype(jnp.int32)
    ws_ref[0] = jnp.concatenate([w1, w2], axis=0)


def _post(x, mla_out, d_out, mod, wo, g2, wr, br):
    b, s, d = x.shape
    tm = POST_TM
    nt = s // tm
    full = lambda a: pl.BlockSpec(a.shape, lambda bi, i: (0,) * a.ndim)
    tok = lambda w: pl.BlockSpec((1, tm, w), lambda bi, i: (bi, i, 0))
    return pl.pallas_call(
        _post_kernel,
        out_shape=(jax.ShapeDtypeStruct((b, s, d), F32),
                   jax.ShapeDtypeStruct((b * s * ROW_CHUNKS, LANES), F32),
                   jax.ShapeDtypeStruct((b * nt, 4, tm), jnp.int32),
                   jax.ShapeDtypeStruct((b * nt, 2, tm), F32),
                   jax.ShapeDtypeStruct((N_EXPERTS, LANES), F32)),
        grid=(b, nt),
        in_specs=[tok(d), tok(mla_out.shape[2]), tok(d_out.shape[2]),
                  pl.BlockSpec((1, 6, d), lambda bi, i: (bi, 0, 0)),
                  full(wo), full(g2), full(wr), full(br)],
        out_specs=(tok(d),
                   pl.BlockSpec((tm * ROW_CHUNKS, LANES), lambda bi, i: (bi * nt + i, 0)),
                   pl.BlockSpec((1, 4, tm), lambda bi, i: (bi * nt + i, 0, 0)),
                   pl.BlockSpec((1, 2, tm), lambda bi, i: (bi * nt + i, 0, 0)),
                   pl.BlockSpec((N_EXPERTS, LANES), lambda bi, i: (0, 0))),
        scratch_shapes=[pltpu.VMEM((N_EXPERTS, LANES), F32)],
        compiler_params=_cparams(("arbitrary", "arbitrary")),
        name="post",
    )(x, mla_out, d_out, mod, wo, g2, wr, br)


def _dispatch_kernel(d0_ref, d1_ref, pad_ref, h_ref, xs_ref, zero_ref, sem):
    tm = h_ref.shape[0] // ROW_CHUNKS
    i = pl.program_id(0)

    @pl.when(i == 0)
    def _():
        zero_ref[...] = jnp.zeros_like(zero_ref)
        for e in range(N_EXPERTS):
            pltpu.make_async_copy(zero_ref, _token_tiles(xs_ref, pad_ref[e], EXPERT_TM), sem).start()
        for e in range(N_EXPERTS):
            pltpu.make_async_copy(zero_ref, _token_tiles(xs_ref, pad_ref[e], EXPERT_TM), sem).wait()
        n_rows = xs_ref.shape[0] // ROW_CHUNKS
        for e in range(N_EXPERTS):
            start = pad_ref[N_EXPERTS] + e * EXPERT_TM

            @pl.when(start < n_rows)
            def _():
                pltpu.make_async_copy(zero_ref, _token_tiles(xs_ref, start, EXPERT_TM), sem).start()

        for e in range(N_EXPERTS):
            start = pad_ref[N_EXPERTS] + e * EXPERT_TM

            @pl.when(start < n_rows)
            def _():
                pltpu.make_async_copy(zero_ref, _token_tiles(xs_ref, start, EXPERT_TM), sem).wait()

    base = i * tm

    def issue(r, carry):
        src = _token_tile(h_ref, r)
        pltpu.make_async_copy(src, _token_tile(xs_ref, d0_ref[base + r]), sem).start(priority=0)
        pltpu.make_async_copy(src, _token_tile(xs_ref, d1_ref[base + r]), sem).start(priority=1)
        return carry

    lax.fori_loop(0, tm, issue, 0, unroll=8)

    def drain(r, carry):
        src = _token_tile(h_ref, r)
        pltpu.make_async_copy(src, _token_tile(xs_ref, 0), sem).wait()
        pltpu.make_async_copy(src, _token_tile(xs_ref, 0), sem).wait()
        return carry

    lax.fori_loop(0, tm, drain, 0, unroll=8)


def _dispatch(dest0, dest1, pad_start, h2, rows):
    t = h2.shape[0] // ROW_CHUNKS
    tm = DISPATCH_TM
    return pl.pallas_call(
        _dispatch_kernel,
        out_shape=jax.ShapeDtypeStruct((rows * ROW_CHUNKS, LANES), F32),
        grid_spec=pltpu.PrefetchScalarGridSpec(
            num_scalar_prefetch=3,
            grid=(t // tm,),
            in_specs=[pl.BlockSpec((tm * ROW_CHUNKS, LANES), lambda i, *_: (i, 0))],
            out_specs=pl.BlockSpec(memory_space=pl.ANY),
            scratch_shapes=[pltpu.VMEM((EXPERT_TM * ROW_CHUNKS, LANES), F32), pltpu.SemaphoreType.DMA]),
        compiler_params=_cparams(("arbitrary",)),
        name="dispatch",
    )(dest0, dest1, pad_start, h2)


def _expert_kernel(be_ref, nbu_ref, x_ref, wg_ref, wu_ref, wd_ref, y_ref):
    @pl.when(pl.program_id(0) < nbu_ref[0])
    def _():
        x = jnp.concatenate(_load_token_rows(x_ref), axis=-1).astype(BF16)
        g = jnp.dot(x, wg_ref[0], preferred_element_type=F32)
        u = jnp.dot(x, wu_ref[0], preferred_element_type=F32)
        a = (g / (1.0 + jnp.exp(-g))) * u
        y = jnp.dot(a.astype(BF16), wd_ref[0], preferred_element_type=F32)
        _store_token_rows(y_ref, y)

    @pl.when(pl.program_id(0) >= nbu_ref[0])
    def _():
        y_ref[...] = jnp.zeros_like(y_ref)


def _experts(block_expert, nb_used, xs, wg, wu, wd):
    rows = xs.shape[0] // ROW_CHUNKS
    d = D_MODEL
    tm = EXPERT_TM
    nb = rows // tm
    row_map = lambda j, be, nbu: (jnp.minimum(j, nbu[0] - 1), 0)
    w_map = lambda j, be, nbu: (be[j], 0, 0)
    return pl.pallas_call(
        _expert_kernel,
        out_shape=jax.ShapeDtypeStruct((rows * ROW_CHUNKS, LANES), F32),
        grid_spec=pltpu.PrefetchScalarGridSpec(
            num_scalar_prefetch=2,
            grid=(nb,),
            in_specs=[pl.BlockSpec((tm * ROW_CHUNKS, LANES), row_map),
                      pl.BlockSpec((1, d, D_EXPERT), w_map),
                      pl.BlockSpec((1, d, D_EXPERT), w_map),
                      pl.BlockSpec((1, D_EXPERT, d), w_map)],
            out_specs=pl.BlockSpec((tm * ROW_CHUNKS, LANES), lambda j, be, nbu: (j, 0))),
        compiler_params=_cparams(("arbitrary",)),
        name="experts",
    )(block_expert, nb_used, xs, wg, wu, wd)


def _combine_kernel(d0_ref, d1_ref, x1_ref, w_ref, mod_ref, gf_ref, ys_ref, o_ref, y0_ref, y1_ref, sems):
    tm = x1_ref.shape[0]
    i = pl.program_id(0)
    n = pl.num_programs(0)

    def gather(step, slot, start):
        base = step * tm

        def one(r, carry):
            c0 = pltpu.make_async_copy(_token_tile(ys_ref, d0_ref[base + r] if start else 0),
                                       _token_tile(y0_ref.at[slot], r), sems.at[slot])
            c1 = pltpu.make_async_copy(_token_tile(ys_ref, d1_ref[base + r] if start else 0),
                                       _token_tile(y1_ref.at[slot], r), sems.at[slot])
            if start:
                c0.start(priority=0)
                c1.start(priority=1)
            else:
                c0.wait()
                c1.wait()
            return carry

        lax.fori_loop(0, tm, one, 0, unroll=8)

    slot = i % 2

    @pl.when(i == 0)
    def _():
        gather(0, 0, True)

    @pl.when(i + 1 < n)
    def _():
        gather(i + 1, 1 - slot, True)

    gather(i, slot, False)

    gate2 = mod_ref[0, 5:6, :]
    w = w_ref[...]
    w0 = w[:, 0:1]
    w1 = w[:, 1:2]
    y0 = _load_token_rows(y0_ref.at[slot])
    y1 = _load_token_rows(y1_ref.at[slot])
    moe = jnp.concatenate([w0 * a + w1 * b for a, b in zip(y0, y1)], axis=-1)
    x2 = x1_ref[...] + gate2 * moe
    o_ref[...] = _rms(x2, gf_ref[...])


def _combine(dest0, dest1, x1, w_tok, mod, gf, ys, s_len):
    t, d = x1.shape
    tm = COMBINE_TM
    per_batch = s_len // tm
    return pl.pallas_call(
        _combine_kernel,
        out_shape=jax.ShapeDtypeStruct((t, d), F32),
        grid_spec=pltpu.PrefetchScalarGridSpec(
            num_scalar_prefetch=2,
            grid=(t // tm,),
            in_specs=[pl.BlockSpec((tm, d), lambda i, *_: (i, 0)),
                      pl.BlockSpec((tm, TOP_K), lambda i, *_: (i, 0)),
                      pl.BlockSpec((1, 6, d), lambda i, *_: (i // per_batch, 0, 0)),
                      pl.BlockSpec((1, d), lambda i, *_: (0, 0)),
                      pl.BlockSpec(memory_space=pl.ANY)],
            out_specs=pl.BlockSpec((tm, d), lambda i, *_: (i, 0)),
            scratch_shapes=[pltpu.VMEM((2, tm * ROW_CHUNKS, LANES), F32),
                            pltpu.VMEM((2, tm * ROW_CHUNKS, LANES), F32),
                            pltpu.SemaphoreType.DMA((2,))]),
        compiler_params=_cparams(("arbitrary",)),
        name="combine",
    )(dest0, dest1, x1, w_tok, mod, gf, ys)


def _rope_tables(s_len):
    half = MLA_ROPE_DIM // 2
    inv = ROPE_THETA ** (-jnp.arange(0, MLA_ROPE_DIM, 2, dtype=F32) / MLA_ROPE_DIM)
    ang = jnp.arange(s_len, dtype=jnp.int32).astype(F32)[:, None] * inv[None, :]
    cos, sin = jnp.cos(ang), jnp.sin(ang)
    reps = LANES // MLA_ROPE_DIM
    cos_t = jnp.tile(jnp.concatenate([cos, cos], axis=1), (1, reps))
    sin_t = jnp.tile(jnp.concatenate([-sin, sin], axis=1), (1, reps))
    del half
    return cos_t, sin_t


def _layer(l, x, mod, norm1_g, w_in, q_a_norm_g, w_q_up, kv_a_norm_g, w_kv_up, lambda_q1, lambda_k1,
           lambda_q2, lambda_k2, subln_g, w_out, norm2_g, w_router_group, b_router_group, w_router_expert,
           b_router_expert, w_expert_gate, w_expert_up, w_expert_down, final_norm_g, cos_t, sin_t):
    b, s, d = x.shape
    t = b * s

    wi = w_in[l]
    win = jnp.concatenate([wi[:, 0:512], wi[:, 576:2112], wi[:, 512:576], jnp.zeros((d, 64), F32)],
                          axis=1).astype(BF16)
    wq4 = w_q_up[l].reshape(MLA_Q_RANK, MLA_HEADS, MLA_NOPE_DIM + MLA_ROPE_DIM)
    wq_pe = jnp.pad(wq4[:, :, MLA_NOPE_DIM:], ((0, 0), (0, 0), (0, 64)))
    wq = jnp.concatenate([wq4[:, :, :MLA_NOPE_DIM].reshape(MLA_Q_RANK, 512),
                          wq_pe.reshape(MLA_Q_RANK, 512)], axis=1).astype(BF16)
    wkv4 = w_kv_up[l].reshape(MLA_KV_RANK, MLA_HEADS, MLA_NOPE_DIM + MLA_V_DIM)
    wkv = jnp.concatenate([wkv4[:, :, :MLA_NOPE_DIM].reshape(MLA_KV_RANK, 512),
                           wkv4[:, :, MLA_NOPE_DIM:].reshape(MLA_KV_RANK, 512)], axis=1).astype(BF16)

    qm, km, vm, dq, dk, dv = _pre(x, mod, norm1_g[l][None], win, q_a_norm_g[l][None], wq,
                                  kv_a_norm_g[l][None], wkv, cos_t, sin_t)

    mla_out = _mla_attn(qm, km, vm)
    lambda_init = 0.8 - 0.6 * math.exp(-0.3 * l)
    lam_vecs = jnp.stack([lambda_q1[l], lambda_k1[l], lambda_q2[l], lambda_k2[l]]).astype(F32)
    d_out = _diff_attn(lam_vecs, subln_g[l][None], dq, dk, dv, lambda_init)

    wr = jnp.concatenate([w_router_expert[l].T, w_router_group[l].T,
                          jnp.zeros((ROUTER_ROWS - N_EXPERTS - N_GROUPS, d), F32)], axis=0)
    br = jnp.concatenate([b_router_expert[l], b_router_group[l],
                          jnp.zeros((ROUTER_ROWS - N_EXPERTS - N_GROUPS,), F32)])[:, None]
    x1, h2, ids, ws, cnt = _post(x, mla_out, d_out, mod, w_out[l].astype(BF16), norm2_g[l][None], wr, br)

    tmx = EXPERT_TM
    counts = cnt[:, 0].astype(jnp.int32)
    padded = ((counts + tmx - 1) // tmx) * tmx
    pends = jnp.cumsum(padded)
    pstarts = pends - padded
    rows = t * TOP_K + N_EXPERTS * tmx
    nb = rows // tmx
    block_row = jnp.arange(nb, dtype=jnp.int32) * tmx
    block_expert = jnp.minimum(jnp.sum((pends[None, :] <= block_row[:, None]).astype(jnp.int32), axis=1),
                               N_EXPERTS - 1).astype(jnp.int32)
    nb_used = (pends[-1:] // tmx).astype(jnp.int32)
    pad_start = jnp.concatenate([pstarts + counts, pends[-1:]]).astype(jnp.int32)

    ids_t = ids.transpose(1, 0, 2).reshape(4, t)
    dest0 = (pstarts[ids_t[0]] + ids_t[2]).astype(jnp.int32)
    dest1 = (pstarts[ids_t[1]] + ids_t[3]).astype(jnp.int32)
    w_tok = ws.transpose(0, 2, 1).reshape(t, TOP_K)

    xs = _dispatch(dest0, dest1, pad_start, h2, rows)
    ys = _experts(block_expert, nb_used, xs, w_expert_gate[l].astype(BF16), w_expert_up[l].astype(BF16),
                  w_expert_down[l].astype(BF16))
    out = _combine(dest0, dest1, x1.reshape(t, d), w_tok, mod, final_norm_g, ys, s)
    return out.reshape(b, s, d)


def kernel(x, c, w_ada, b_ada, norm1_g, w_in, q_a_norm_g, w_q_up, kv_a_norm_g, w_kv_up, lambda_q1, lambda_k1,
           lambda_q2, lambda_k2, subln_g, w_out, norm2_g, w_router_group, b_router_group, w_router_expert,
           b_router_expert, w_expert_gate, w_expert_up, w_expert_down, final_norm_g):
    b, s, d = x.shape
    depth = w_ada.shape[0]
    assert depth == 1, "the final RMSNorm is fused into the last layer's combine step"
    cos_t, sin_t = _rope_tables(s)
    mod = _ada(c, w_ada[0], b_ada[0]).reshape(b, 6, d)
    return _layer(0, x, mod, norm1_g, w_in, q_a_norm_g, w_q_up, kv_a_norm_g, w_kv_up, lambda_q1, lambda_k1,
                  lambda_q2, lambda_k2, subln_g, w_out, norm2_g, w_router_group, b_router_group,
                  w_router_expert, b_router_expert, w_expert_gate, w_expert_up, w_expert_down,
                  final_norm_g[None], cos_t, sin_t)
```

```python
import functools
import math

import jax
import jax.numpy as jnp
from jax import lax
from jax.experimental import pallas as pl
from jax.experimental.pallas import tpu as pltpu

D_MODEL = 1024
ROPE_THETA = 10000.0
EPS = 1e-6

MLA_HEADS = 4
MLA_Q_RANK = 256
MLA_KV_RANK = 256
MLA_NOPE_DIM = 128
MLA_ROPE_DIM = 64
MLA_V_DIM = 128
MLA_QK_PAD = 256

DIFF_HEADS = 4
DIFF_QK_DIM = 64
DIFF_V_DIM = 128

N_GROUPS = 4
EXPERTS_PER_GROUP = 8
N_EXPERTS = 32
TOP_K = 2
D_EXPERT = 512

LANES = 128
ROW_CHUNKS = D_MODEL // LANES
ROUTER_ROWS = 40

VMEM_LIMIT = 56 * 1024 * 1024

PRE_TM = 512
ATT_TQ = 512
ATT_TK = 1024
ATT_SLOTS = 3
ATT_HEADS_PER_STEP = 2
LOG2E = math.log2(math.e)
POST_TM = 512
DISPATCH_TM = 1024
EXPERT_TM = 512
COMBINE_TM = 512

F32 = jnp.float32
BF16 = jnp.bfloat16
NEG_INF = float("-inf")


def _cparams(sem):
    return pltpu.CompilerParams(dimension_semantics=sem, vmem_limit_bytes=VMEM_LIMIT)


def _ada_kernel(c_ref, w_ref, b_ref, o_ref):
    c = c_ref[...]
    sc = c / (1.0 + jnp.exp(-c))
    o_ref[...] = jnp.dot(sc, w_ref[...], preferred_element_type=F32,
                         precision=lax.Precision.HIGHEST) + b_ref[...]


def _ada(c, w_ada, b_ada):
    b, d = c.shape
    n = w_ada.shape[1]
    tn = 1536
    return pl.pallas_call(
        _ada_kernel,
        out_shape=jax.ShapeDtypeStruct((b, n), F32),
        grid=(n // tn,),
        in_specs=[pl.BlockSpec((b, d), lambda j: (0, 0)),
                  pl.BlockSpec((d, tn), lambda j: (0, j)),
                  pl.BlockSpec((1, tn), lambda j: (0, j))],
        out_specs=pl.BlockSpec((b, tn), lambda j: (0, j)),
        compiler_params=_cparams(("parallel",)),
        name="ada",
    )(c, w_ada, b_ada.reshape(1, n))


def _rms(x, g):
    return x * lax.rsqrt(jnp.mean(x * x, axis=-1, keepdims=True) + EPS) * g


def _token_tile(ref, t):
    return ref.at[pl.ds(pl.multiple_of(t * ROW_CHUNKS, ROW_CHUNKS), ROW_CHUNKS), :]


def _token_tiles(ref, t, n):
    return ref.at[pl.ds(pl.multiple_of(t * ROW_CHUNKS, ROW_CHUNKS), n * ROW_CHUNKS), :]


def _load_token_rows(ref):
    tm = ref.shape[0] // ROW_CHUNKS
    return [ref[pl.ds(c, tm, stride=ROW_CHUNKS), :] for c in range(ROW_CHUNKS)]


def _store_token_rows(ref, val):
    tm = ref.shape[0] // ROW_CHUNKS
    for c in range(ROW_CHUNKS):
        ref[pl.ds(c, tm, stride=ROW_CHUNKS), :] = val[:, c * LANES:(c + 1) * LANES]


def _rope_tile(x, cos, sin_signed, first_half):
    fwd = pltpu.roll(x, 96, axis=1)
    bwd = pltpu.roll(x, 32, axis=1)
    return x * cos + jnp.where(first_half, fwd, bwd) * sin_signed


def _pre_kernel(x_ref, mod_ref, g1_ref, win_ref, gq_ref, wq_ref, gkv_ref, wkv_ref, cos_ref, sin_ref,
                qm_ref, km_ref, vm_ref, dq_ref, dk_ref, dv_ref):
    x = x_ref[0]
    shift1 = mod_ref[0, 0:1, :]
    scale1 = mod_ref[0, 1:2, :]
    h = _rms(x, g1_ref[...]) * (1.0 + scale1) + shift1
    z = jnp.dot(h.astype(BF16), win_ref[...], preferred_element_type=F32)

    cos = cos_ref[...]
    sin = sin_ref[...]
    lane = lax.broadcasted_iota(jnp.int32, cos.shape, 1)
    first_half = (lane % 64) < 32

    cq = _rms(z[:, 0:256], gq_ref[...])
    q = jnp.dot(cq.astype(BF16), wq_ref[...], preferred_element_type=F32)
    ckv = _rms(z[:, 256:512], gkv_ref[...])
    kv = jnp.dot(ckv.astype(BF16), wkv_ref[...], preferred_element_type=F32)
    kpe = _rope_tile(z[:, 2048:2176], cos, sin, first_half).astype(BF16)

    q_scale = LOG2E * (MLA_NOPE_DIM + MLA_ROPE_DIM) ** -0.5
    for hd in range(MLA_HEADS):
        qn = q[:, hd * 128:(hd + 1) * 128] * q_scale
        qp = _rope_tile(q[:, 512 + hd * 128:512 + (hd + 1) * 128], cos, sin, first_half) * q_scale
        qm_ref[0, hd, :, 0:128] = qn.astype(BF16)
        qm_ref[0, hd, :, 128:256] = qp.astype(BF16)
        km_ref[0, hd, :, 0:128] = kv[:, hd * 128:(hd + 1) * 128].astype(BF16)
        km_ref[0, hd, :, 128:256] = kpe
        vm_ref[0, hd] = kv[:, 512 + hd * 128:512 + (hd + 1) * 128].astype(BF16)

    d_scale = LOG2E * DIFF_QK_DIM ** -0.5
    for hd in range(DIFF_HEADS):
        dq = _rope_tile(z[:, 512 + hd * 128:512 + (hd + 1) * 128], cos, sin, first_half) * d_scale
        dk = _rope_tile(z[:, 1024 + hd * 128:1024 + (hd + 1) * 128], cos, sin, first_half)
        dq_ref[0, hd] = dq.astype(BF16)
        dk_ref[0, hd] = dk.astype(BF16)
        dv_ref[0, hd] = z[:, 1536 + hd * 128:1536 + (hd + 1) * 128].astype(BF16)


def _pre(x, mod, g1, win, gq, wq, gkv, wkv, cos_t, sin_t):
    b, s, d = x.shape
    tm = PRE_TM
    full = lambda a: pl.BlockSpec(a.shape, lambda bi, i: (0,) * a.ndim)
    head_out = lambda w: pl.BlockSpec((1, 4, tm, w), lambda bi, i: (bi, 0, i, 0))
    head_shape = lambda w: jax.ShapeDtypeStruct((b, 4, s, w), BF16)
    return pl.pallas_call(
        _pre_kernel,
        out_shape=(head_shape(256), head_shape(256), head_shape(128),
                   head_shape(128), head_shape(128), head_shape(128)),
        grid=(b, s // tm),
        in_specs=[pl.BlockSpec((1, tm, d), lambda bi, i: (bi, i, 0)),
                  pl.BlockSpec((1, 6, d), lambda bi, i: (bi, 0, 0)),
                  full(g1), full(win), full(gq), full(wq), full(gkv), full(wkv),
                  pl.BlockSpec((tm, LANES), lambda bi, i: (i, 0)),
                  pl.BlockSpec((tm, LANES), lambda bi, i: (i, 0))],
        out_specs=(head_out(256), head_out(256), head_out(128),
                   head_out(128), head_out(128), head_out(128)),
        compiler_params=_cparams(("parallel", "parallel")),
        name="pre",
    )(x, mod, g1, win, gq, wq, gkv, wkv, cos_t, sin_t)


def _softmax_pv(s, vx, m, accx):
    m_new = jnp.maximum(m, jnp.max(s, axis=-1, keepdims=True))
    alpha = jnp.exp2(m - m_new)
    p = jnp.exp2(s - m_new)
    accx_new = alpha * accx + jnp.dot(p.astype(BF16), vx, preferred_element_type=F32)
    return m_new, accx_new


def _attend(streams, n_chunks, s_ref):
    steps = [(si, c) for si in range(len(streams)) for c in range(n_chunks)]
    ahead = ATT_SLOTS - 1
    ones = jnp.ones((ATT_TK, LANES), BF16)

    def scores_into(idx):
        si, c = steps[idx]
        q, k_chunk, _ = streams[si]
        s_ref[idx % ATT_SLOTS] = lax.dot_general(q, k_chunk(c), (((1,), (1,)), ((), ())),
                                                 preferred_element_type=F32)

    for idx in range(min(ahead, len(steps))):
        scores_into(idx)
    outs = []
    state = None
    for idx, (si, c) in enumerate(steps):
        q, _, v_chunk = streams[si]
        if c == 0:
            state = (jnp.full((q.shape[0], 1), NEG_INF, F32), jnp.zeros((q.shape[0], 2 * LANES), F32))
        if idx + ahead < len(steps):
            scores_into(idx + ahead)
        vx = jnp.concatenate([v_chunk(c), ones], axis=-1)
        state = _softmax_pv(s_ref[idx % ATT_SLOTS], vx, *state)
        if c == n_chunks - 1:
            outs.append(state[1])
    return outs


def _chunk_of(ref, head):
    return lambda c: ref[0, head, c * ATT_TK:(c + 1) * ATT_TK, :]


def _normalise(accx):
    return accx[:, :LANES] / accx[:, LANES:]


def _mla_kernel(q_ref, k_ref, v_ref, o_ref, s_ref):
    n_chunks = k_ref.shape[2] // ATT_TK
    streams = [(q_ref[0, h], _chunk_of(k_ref, h), _chunk_of(v_ref, h)) for h in range(ATT_HEADS_PER_STEP)]
    for h, accx in enumerate(_attend(streams, n_chunks, s_ref)):
        o_ref[0, :, h * MLA_V_DIM:(h + 1) * MLA_V_DIM] = _normalise(accx).astype(o_ref.dtype)


def _mla_attn(q, k, v):
    b, h, s, _ = q.shape
    tq = ATT_TQ
    hp = ATT_HEADS_PER_STEP
    return pl.pallas_call(
        _mla_kernel,
        out_shape=jax.ShapeDtypeStruct((b, s, h * MLA_V_DIM), BF16),
        grid=(b, h // hp, s // tq),
        in_specs=[pl.BlockSpec((1, hp, tq, MLA_QK_PAD), lambda bi, hi, i: (bi, hi, i, 0)),
                  pl.BlockSpec((1, hp, s, MLA_QK_PAD), lambda bi, hi, i: (bi, hi, 0, 0)),
                  pl.BlockSpec((1, hp, s, MLA_V_DIM), lambda bi, hi, i: (bi, hi, 0, 0))],
        out_specs=pl.BlockSpec((1, tq, hp * MLA_V_DIM), lambda bi, hi, i: (bi, i, hi)),
        scratch_shapes=[pltpu.VMEM((ATT_SLOTS, tq, ATT_TK), F32)],
        compiler_params=_cparams(("parallel", "parallel", "parallel")),
        name="mla_attn",
    )(q, k, v)


def _diff_kernel(lam_ref, g_ref, q_ref, k_ref, v_ref, o_ref, s_ref, *, out_scale, lambda_init):
    n_chunks = k_ref.shape[2] // ATT_TK
    streams = []
    for h in range(ATT_HEADS_PER_STEP):
        q = q_ref[0, h]
        lane = lax.broadcasted_iota(jnp.int32, q.shape, 1)
        zero = jnp.zeros_like(q)
        q1 = jnp.where(lane < DIFF_QK_DIM, q, zero)
        q2 = jnp.where(lane < DIFF_QK_DIM, zero, q)
        streams += [(q1, _chunk_of(k_ref, h), _chunk_of(v_ref, h)), (q2, _chunk_of(k_ref, h), _chunk_of(v_ref, h))]
    acc = _attend(streams, n_chunks, s_ref)

    lv = lam_ref[...]
    lam = (jnp.exp(jnp.sum(lv[0:1] * lv[1:2], axis=-1, keepdims=True))
           - jnp.exp(jnp.sum(lv[2:3] * lv[3:4], axis=-1, keepdims=True)) + lambda_init)
    for h in range(ATT_HEADS_PER_STEP):
        o = _normalise(acc[2 * h]) - lam * _normalise(acc[2 * h + 1])
        o_ref[0, :, h * DIFF_V_DIM:(h + 1) * DIFF_V_DIM] = (_rms(o, g_ref[...]) * out_scale).astype(o_ref.dtype)


def _diff_attn(lam_vecs, subln_g, q, k, v, lambda_init):
    b, h, s, _ = q.shape
    tq = ATT_TQ
    hp = ATT_HEADS_PER_STEP
    kern = functools.partial(_diff_kernel, out_scale=1.0 - lambda_init, lambda_init=lambda_init)
    return pl.pallas_call(
        kern,
        out_shape=jax.ShapeDtypeStruct((b, s, h * DIFF_V_DIM), BF16),
        grid=(b, h // hp, s // tq),
        in_specs=[pl.BlockSpec((4, DIFF_QK_DIM), lambda bi, hi, i: (0, 0)),
                  pl.BlockSpec((1, DIFF_V_DIM), lambda bi, hi, i: (0, 0)),
                  pl.BlockSpec((1, hp, tq, 128), lambda bi, hi, i: (bi, hi, i, 0)),
                  pl.BlockSpec((1, hp, s, 128), lambda bi, hi, i: (bi, hi, 0, 0)),
                  pl.BlockSpec((1, hp, s, DIFF_V_DIM), lambda bi, hi, i: (bi, hi, 0, 0))],
        out_specs=pl.BlockSpec((1, tq, hp * DIFF_V_DIM), lambda bi, hi, i: (bi, i, hi)),
        scratch_shapes=[pltpu.VMEM((ATT_SLOTS, tq, ATT_TK), F32)],
        compiler_params=_cparams(("parallel", "parallel", "parallel")),
        name="diff_attn",
    )(lam_vecs, subln_g, q, k, v)


def _post_kernel(x_ref, mla_ref, dif_ref, mod_ref, wo_ref, g2_ref, wr_ref, br_ref,
                 x1_ref, h2_ref, ids_ref, ws_ref, cnt_ref, base_ref):
    first = jnp.logical_and(pl.program_id(0) == 0, pl.program_id(1) == 0)

    @pl.when(first)
    def _():
        base_ref[...] = jnp.zeros_like(base_ref)

    tm = x_ref.shape[1]
    gate1 = mod_ref[0, 2:3, :]
    shift2 = mod_ref[0, 3:4, :]
    scale2 = mod_ref[0, 4:5, :]
    mixin = jnp.concatenate([mla_ref[0], dif_ref[0]], axis=-1)
    mix = jnp.dot(mixin, wo_ref[...], preferred_element_type=F32)
    x1 = x_ref[0] + gate1 * mix
    x1_ref[0] = x1
    h2 = _rms(x1, g2_ref[...]) * (1.0 + scale2) + shift2
    _store_token_rows(h2_ref, h2)

    logits = lax.dot_general(wr_ref[...], h2, (((1,), (1,)), ((), ())), preferred_element_type=F32,
                             precision=lax.Precision.HIGHEST) + br_ref[...]
    el = logits[0:N_EXPERTS]
    gl = logits[N_EXPERTS:N_EXPERTS + N_GROUPS]

    grow = lax.broadcasted_iota(jnp.int32, gl.shape, 0).astype(F32)
    gmax = jnp.max(gl, axis=0, keepdims=True)
    gidx = jnp.min(jnp.where(gl == gmax, grow, float(N_GROUPS)), axis=0, keepdims=True)
    gprob = 1.0 / jnp.sum(jnp.exp(gl - gmax), axis=0, keepdims=True)

    erow_i = lax.broadcasted_iota(jnp.int32, el.shape, 0)
    erow = erow_i.astype(F32)
    egrp = (erow_i // EXPERTS_PER_GROUP).astype(F32)
    masked = jnp.where(egrp == gidx, el, NEG_INF)
    l1 = jnp.max(masked, axis=0, keepdims=True)
    i1 = jnp.min(jnp.where(masked == l1, erow, float(N_EXPERTS)), axis=0, keepdims=True)
    oh1 = erow == i1
    masked2 = jnp.where(oh1, NEG_INF, masked)
    l2 = jnp.max(masked2, axis=0, keepdims=True)
    i2 = jnp.min(jnp.where(masked2 == l2, erow, float(N_EXPERTS)), axis=0, keepdims=True)
    oh2 = erow == i2
    e21 = jnp.exp(l2 - l1)
    w1 = gprob * (1.0 / (1.0 + e21))
    w2 = gprob * (e21 / (1.0 + e21))

    oh = jnp.where(jnp.logical_or(oh1, oh2), 1.0, 0.0)
    tr = lax.broadcasted_iota(jnp.int32, (tm, tm), 0)
    tc = lax.broadcasted_iota(jnp.int32, (tm, tm), 1)
    upper = jnp.where(tr < tc, 1.0, 0.0).astype(BF16)
    pos = base_ref[...][:, 0:1] + jnp.dot(oh.astype(BF16), upper, preferred_element_type=F32)
    r1 = jnp.sum(jnp.where(oh1, pos, 0.0), axis=0, keepdims=True)
    r2 = jnp.sum(jnp.where(oh2, pos, 0.0), axis=0, keepdims=True)
    base_new = base_ref[...] + jnp.sum(oh, axis=1, keepdims=True)
    base_ref[...] = base_new
    cnt_ref[...] = base_new

    ids_ref[0] = jnp.concatenate([i1, i2, r1, r2], axis=0).astype(jnp.int32)
    ws_ref[0] = jnp.concatenate([w1, w2], axis=0)


def _post(x, mla_out, d_out, mod, wo, g2, wr, br):
    b, s, d = x.shape
    tm = POST_TM
    nt = s // tm
    full = lambda a: pl.BlockSpec(a.shape, lambda bi, i: (0,) * a.ndim)
    tok = lambda w: pl.BlockSpec((1, tm, w), lambda bi, i: (bi, i, 0))
    return pl.pallas_call(
        _post_kernel,
        out_shape=(jax.ShapeDtypeStruct((b, s, d), F32),
                   jax.ShapeDtypeStruct((b * s * ROW_CHUNKS, LANES), F32),
                   jax.ShapeDtypeStruct((b * nt, 4, tm), jnp.int32),
                   jax.ShapeDtypeStruct((b * nt, 2, tm), F32),
                   jax.ShapeDtypeStruct((N_EXPERTS, LANES), F32)),
        grid=(b, nt),
        in_specs=[tok(d), tok(mla_out.shape[2]), tok(d_out.shape[2]),
                  pl.BlockSpec((1, 6, d), lambda bi, i: (bi, 0, 0)),
                  full(wo), full(g2), full(wr), full(br)],
        out_specs=(tok(d),
                   pl.BlockSpec((tm * ROW_CHUNKS, LANES), lambda bi, i: (bi * nt + i, 0)),
                   pl.BlockSpec((1, 4, tm), lambda bi, i: (bi * nt + i, 0, 0)),
                   pl.BlockSpec((1, 2, tm), lambda bi, i: (bi * nt + i, 0, 0)),
                   pl.BlockSpec((N_EXPERTS, LANES), lambda bi, i: (0, 0))),
        scratch_shapes=[pltpu.VMEM((N_EXPERTS, LANES), F32)],
        compiler_params=_cparams(("arbitrary", "arbitrary")),
        name="post",
    )(x, mla_out, d_out, mod, wo, g2, wr, br)


def _dispatch_kernel(d0_ref, d1_ref, pad_ref, h_ref, xs_ref, zero_ref, sem):
    tm = h_ref.shape[0] // ROW_CHUNKS
    i = pl.program_id(0)

    @pl.when(i == 0)
    def _():
        zero_ref[...] = jnp.zeros_like(zero_ref)
        for e in range(N_EXPERTS):
            pltpu.make_async_copy(zero_ref, _token_tiles(xs_ref, pad_ref[e], EXPERT_TM), sem).start()
        for e in range(N_EXPERTS):
            pltpu.make_async_copy(zero_ref, _token_tiles(xs_ref, pad_ref[e], EXPERT_TM), sem).wait()
        n_rows = xs_ref.shape[0] // ROW_CHUNKS
        for e in range(N_EXPERTS):
            start = pad_ref[N_EXPERTS] + e * EXPERT_TM

            @pl.when(start < n_rows)
            def _():
                pltpu.make_async_copy(zero_ref, _token_tiles(xs_ref, start, EXPERT_TM), sem).start()

        for e in range(N_EXPERTS):
            start = pad_ref[N_EXPERTS] + e * EXPERT_TM

            @pl.when(start < n_rows)
            def _():
                pltpu.make_async_copy(zero_ref, _token_tiles(xs_ref, start, EXPERT_TM), sem).wait()

    base = i * tm

    def issue(r, carry):
        src = _token_tile(h_ref, r)
        pltpu.make_async_copy(src, _token_tile(xs_ref, d0_ref[base + r]), sem).start(priority=0)
        pltpu.make_async_copy(src, _token_tile(xs_ref, d1_ref[base + r]), sem).start(priority=1)
        return carry

    lax.fori_loop(0, tm, issue, 0, unroll=8)

    def drain(r, carry):
        src = _token_tile(h_ref, r)
        pltpu.make_async_copy(src, _token_tile(xs_ref, 0), sem).wait()
        pltpu.make_async_copy(src, _token_tile(xs_ref, 0), sem).wait()
        return carry

    lax.fori_loop(0, tm, drain, 0, unroll=8)


def _dispatch(dest0, dest1, pad_start, h2, rows):
    t = h2.shape[0] // ROW_CHUNKS
    tm = DISPATCH_TM
    return pl.pallas_call(
        _dispatch_kernel,
        out_shape=jax.ShapeDtypeStruct((rows * ROW_CHUNKS, LANES), F32),
        grid_spec=pltpu.PrefetchScalarGridSpec(
            num_scalar_prefetch=3,
            grid=(t // tm,),
            in_specs=[pl.BlockSpec((tm * ROW_CHUNKS, LANES), lambda i, *_: (i, 0))],
            out_specs=pl.BlockSpec(memory_space=pl.ANY),
            scratch_shapes=[pltpu.VMEM((EXPERT_TM * ROW_CHUNKS, LANES), F32), pltpu.SemaphoreType.DMA]),
        compiler_params=_cparams(("arbitrary",)),
        name="dispatch",
    )(dest0, dest1, pad_start, h2)


def _expert_kernel(be_ref, nbu_ref, x_ref, wg_ref, wu_ref, wd_ref, y_ref):
    @pl.when(pl.program_id(0) < nbu_ref[0])
    def _():
        x = jnp.concatenate(_load_token_rows(x_ref), axis=-1).astype(BF16)
        g = jnp.dot(x, wg_ref[0], preferred_element_type=F32)
        u = jnp.dot(x, wu_ref[0], preferred_element_type=F32)
        a = (g / (1.0 + jnp.exp(-g))) * u
        y = jnp.dot(a.astype(BF16), wd_ref[0], preferred_element_type=F32)
        _store_token_rows(y_ref, y)

    @pl.when(pl.program_id(0) >= nbu_ref[0])
    def _():
        y_ref[...] = jnp.zeros_like(y_ref)


def _experts(block_expert, nb_used, xs, wg, wu, wd):
    rows = xs.shape[0] // ROW_CHUNKS
    d = D_MODEL
    tm = EXPERT_TM
    nb = rows // tm
    row_map = lambda j, be, nbu: (jnp.minimum(j, nbu[0] - 1), 0)
    w_map = lambda j, be, nbu: (be[j], 0, 0)
    return pl.pallas_call(
        _expert_kernel,
        out_shape=jax.ShapeDtypeStruct((rows * ROW_CHUNKS, LANES), F32),
        grid_spec=pltpu.PrefetchScalarGridSpec(
            num_scalar_prefetch=2,
            grid=(nb,),
            in_specs=[pl.BlockSpec((tm * ROW_CHUNKS, LANES), row_map),
                      pl.BlockSpec((1, d, D_EXPERT), w_map),
                      pl.BlockSpec((1, d, D_EXPERT), w_map),
                      pl.BlockSpec((1, D_EXPERT, d), w_map)],
            out_specs=pl.BlockSpec((tm * ROW_CHUNKS, LANES), lambda j, be, nbu: (j, 0))),
        compiler_params=_cparams(("arbitrary",)),
        name="experts",
    )(block_expert, nb_used, xs, wg, wu, wd)


def _combine_kernel(d0_ref, d1_ref, x1_ref, w_ref, mod_ref, gf_ref, ys_ref, o_ref, y0_ref, y1_ref, sems):
    tm = x1_ref.shape[0]
    i = pl.program_id(0)
    n = pl.num_programs(0)

    def gather(step, slot, start):
        base = step * tm

        def one(r, carry):
            c0 = pltpu.make_async_copy(_token_tile(ys_ref, d0_ref[base + r] if start else 0),
                                       _token_tile(y0_ref.at[slot], r), sems.at[slot])
            c1 = pltpu.make_async_copy(_token_tile(ys_ref, d1_ref[base + r] if start else 0),
                                       _token_tile(y1_ref.at[slot], r), sems.at[slot])
            if start:
                c0.start(priority=0)
                c1.start(priority=1)
            else:
                c0.wait()
                c1.wait()
            return carry

        lax.fori_loop(0, tm, one, 0, unroll=8)

    slot = i % 2

    @pl.when(i == 0)
    def _():
        gather(0, 0, True)

    @pl.when(i + 1 < n)
    def _():
        gather(i + 1, 1 - slot, True)

    gather(i, slot, False)

    gate2 = mod_ref[0, 5:6, :]
    w = w_ref[...]
    w0 = w[:, 0:1]
    w1 = w[:, 1:2]
    y0 = _load_token_rows(y0_ref.at[slot])
    y1 = _load_token_rows(y1_ref.at[slot])
    moe = jnp.concatenate([w0 * a + w1 * b for a, b in zip(y0, y1)], axis=-1)
    x2 = x1_ref[...] + gate2 * moe
    o_ref[...] = _rms(x2, gf_ref[...])


def _combine(dest0, dest1, x1, w_tok, mod, gf, ys, s_len):
    t, d = x1.shape
    tm = COMBINE_TM
    per_batch = s_len // tm
    return pl.pallas_call(
        _combine_kernel,
        out_shape=jax.ShapeDtypeStruct((t, d), F32),
        grid_spec=pltpu.PrefetchScalarGridSpec(
            num_scalar_prefetch=2,
            grid=(t // tm,),
            in_specs=[pl.BlockSpec((tm, d), lambda i, *_: (i, 0)),
                      pl.BlockSpec((tm, TOP_K), lambda i, *_: (i, 0)),
                      pl.BlockSpec((1, 6, d), lambda i, *_: (i // per_batch, 0, 0)),
                      pl.BlockSpec((1, d), lambda i, *_: (0, 0)),
                      pl.BlockSpec(memory_space=pl.ANY)],
            out_specs=pl.BlockSpec((tm, d), lambda i, *_: (i, 0)),
            scratch_shapes=[pltpu.VMEM((2, tm * ROW_CHUNKS, LANES), F32),
                            pltpu.VMEM((2, tm * ROW_CHUNKS, LANES), F32),
                            pltpu.SemaphoreType.DMA((2,))]),
        compiler_params=_cparams(("arbitrary",)),
        name="combine",
    )(dest0, dest1, x1, w_tok, mod, gf, ys)


def _rope_tables(s_len):
    half = MLA_ROPE_DIM // 2
    inv = ROPE_THETA ** (-jnp.arange(0, MLA_ROPE_DIM, 2, dtype=F32) / MLA_ROPE_DIM)
    ang = jnp.arange(s_len, dtype=jnp.int32).astype(F32)[:, None] * inv[None, :]
    cos, sin = jnp.cos(ang), jnp.sin(ang)
    reps = LANES // MLA_ROPE_DIM
    cos_t = jnp.tile(jnp.concatenate([cos, cos], axis=1), (1, reps))
    sin_t = jnp.tile(jnp.concatenate([-sin, sin], axis=1), (1, reps))
    del half
    return cos_t, sin_t


def _layer(l, x, mod, norm1_g, w_in, q_a_norm_g, w_q_up, kv_a_norm_g, w_kv_up, lambda_q1, lambda_k1,
           lambda_q2, lambda_k2, subln_g, w_out, norm2_g, w_router_group, b_router_group, w_router_expert,
           b_router_expert, w_expert_gate, w_expert_up, w_expert_down, final_norm_g, cos_t, sin_t):
    b, s, d = x.shape
    t = b * s

    wi = w_in[l]
    win = jnp.concatenate([wi[:, 0:512], wi[:, 576:2112], wi[:, 512:576], jnp.zeros((d, 64), F32)],
                          axis=1).astype(BF16)
    wq4 = w_q_up[l].reshape(MLA_Q_RANK, MLA_HEADS, MLA_NOPE_DIM + MLA_ROPE_DIM)
    wq_pe = jnp.pad(wq4[:, :, MLA_NOPE_DIM:], ((0, 0), (0, 0), (0, 64)))
    wq = jnp.concatenate([wq4[:, :, :MLA_NOPE_DIM].reshape(MLA_Q_RANK, 512),
                          wq_pe.reshape(MLA_Q_RANK, 512)], axis=1).astype(BF16)
    wkv4 = w_kv_up[l].reshape(MLA_KV_RANK, MLA_HEADS, MLA_NOPE_DIM + MLA_V_DIM)
    wkv = jnp.concatenate([wkv4[:, :, :MLA_NOPE_DIM].reshape(MLA_KV_RANK, 512),
                           wkv4[:, :, MLA_NOPE_DIM:].reshape(MLA_KV_RANK, 512)], axis=1).astype(BF16)

    qm, km, vm, dq, dk, dv = _pre(x, mod, norm1_g[l][None], win, q_a_norm_g[l][None], wq,
                                  kv_a_norm_g[l][None], wkv, cos_t, sin_t)

    mla_out = _mla_attn(qm, km, vm)
    lambda_init = 0.8 - 0.6 * math.exp(-0.3 * l)
    lam_vecs = jnp.stack([lambda_q1[l], lambda_k1[l], lambda_q2[l], lambda_k2[l]]).astype(F32)
    d_out = _diff_attn(lam_vecs, subln_g[l][None], dq, dk, dv, lambda_init)

    wr = jnp.concatenate([w_router_expert[l].T, w_router_group[l].T,
                          jnp.zeros((ROUTER_ROWS - N_EXPERTS - N_GROUPS, d), F32)], axis=0)
    br = jnp.concatenate([b_router_expert[l], b_router_group[l],
                          jnp.zeros((ROUTER_ROWS - N_EXPERTS - N_GROUPS,), F32)])[:, None]
    x1, h2, ids, ws, cnt = _post(x, mla_out, d_out, mod, w_out[l].astype(BF16), norm2_g[l][None], wr, br)

    tmx = EXPERT_TM
    counts = cnt[:, 0].astype(jnp.int32)
    padded = ((counts + tmx - 1) // tmx) * tmx
    pends = jnp.cumsum(padded)
    pstarts = pends - padded
    rows = t * TOP_K + N_EXPERTS * tmx
    nb = rows // tmx
    block_row = jnp.arange(nb, dtype=jnp.int32) * tmx
    block_expert = jnp.minimum(jnp.sum((pends[None, :] <= block_row[:, None]).astype(jnp.int32), axis=1),
                               N_EXPERTS - 1).astype(jnp.int32)
    nb_used = (pends[-1:] // tmx).astype(jnp.int32)
    pad_start = jnp.concatenate([pstarts + counts, pends[-1:]]).astype(jnp.int32)

    ids_t = ids.transpose(1, 0, 2).reshape(4, t)
    dest0 = (pstarts[ids_t[0]] + ids_t[2]).astype(jnp.int32)
    dest1 = (pstarts[ids_t[1]] + ids_t[3]).astype(jnp.int32)
    w_tok = ws.transpose(0, 2, 1).reshape(t, TOP_K)

    xs = _dispatch(dest0, dest1, pad_start, h2, rows)
    ys = _experts(block_expert, nb_used, xs, w_expert_gate[l].astype(BF16), w_expert_up[l].astype(BF16),
                  w_expert_down[l].astype(BF16))
    out = _combine(dest0, dest1, x1.reshape(t, d), w_tok, mod, final_norm_g, ys, s)
    return out.reshape(b, s, d)


def kernel(x, c, w_ada, b_ada, norm1_g, w_in, q_a_norm_g, w_q_up, kv_a_norm_g, w_kv_up, lambda_q1, lambda_k1,
           lambda_q2, lambda_k2, subln_g, w_out, norm2_g, w_router_group, b_router_group, w_router_expert,
           b_router_expert, w_expert_gate, w_expert_up, w_expert_down, final_norm_g):
    b, s, d = x.shape
    depth = w_ada.shape[0]
    assert depth == 1, "the final RMSNorm is fused into the last layer's combine step"
    cos_t, sin_t = _rope_tables(s)
    mod = _ada(c, w_ada[0], b_ada[0]).reshape(b, 6, d)
    return _layer(0, x, mod, norm1_g, w_in, q_a_norm_g, w_q_up, kv_a_norm_g, w_kv_up, lambda_q1, lambda_k1,
                  lambda_q2, lambda_k2, subln_g, w_out, norm2_g, w_router_group, b_router_group,
                  w_router_expert, b_router_expert, w_expert_gate, w_expert_up, w_expert_down,
                  final_norm_g[None], cos_t, sin_t)
```
